```python
import functools
import jax, jax.numpy as jnp
from jax import lax
import numpy as np

D_MODEL = 2048
BATCH = 4
SEQ = 2048
DEPTH = 2
DEC_BATCH = 32
DEC_SEQ = 1
PAST_LEN = 8192
PAGE_SIZE = 128

H_A = 4
DK_HEAD_A = D_MODEL // 16
DV_HEAD_A = D_MODEL // 8
DK_A = H_A * DK_HEAD_A
DV_A = H_A * DV_HEAD_A
GATE_RANK = 16
GATE_TAU = 16.0
GLA_CHUNK = 64
C_CONV = D_MODEL // 2
CONV_K = 31
H_F = 8
DH_F = D_MODEL // 16
D_F = H_F * DH_F
Q_BLOCK = 128
FORGET_BIAS = 3.0
D_FF = ((8 * D_MODEL // 3 + 127) // 128) * 128
N_BRANCH = 3
EPS = 1e-6
IN_SPLITS = (DK_A, DK_A, DV_A, DV_A, GATE_RANK, 2 * C_CONV, D_F, D_F, D_F, H_F, N_BRANCH * D_MODEL)
D_IN = sum(IN_SPLITS)

kernel_name = "hybrid_gla_conformer_fox_decoder_step"


def rms_norm(x, g):
    xf = x.astype(jnp.float32)
    y = xf * lax.rsqrt(jnp.mean(xf * xf, axis=-1, keepdims=True) + EPS)
    return (y * g.astype(jnp.float32)).astype(x.dtype)


def layer_norm(x, g, b):
    xf = x.astype(jnp.float32)
    mu = jnp.mean(xf, axis=-1, keepdims=True)
    var = jnp.mean(jnp.square(xf - mu), axis=-1, keepdims=True)
    y = (xf - mu) * lax.rsqrt(var + EPS)
    return (y * g.astype(jnp.float32) + b.astype(jnp.float32)).astype(x.dtype)


def half_ffn(x, g_pre, g_post, w_gate, w_up, w_down):
    h = rms_norm(x, g_pre)
    f = (jax.nn.silu(h @ w_gate) * (h @ w_up)) @ w_down
    return x + 0.5 * rms_norm(f, g_post)


def split_in(z):
    parts, off = [], 0
    for n in IN_SPLITS:
        parts.append(z[..., off:off + n])
        off += n
    return parts


def gla_chunk(S, q, k, v, la):
    L = q.shape[1]
    b = jnp.cumsum(la, axis=1)
    o_inter = jnp.einsum('blhk,bhkv->blhv', q * jnp.exp(b), S)
    causal = jnp.tril(jnp.ones((L, L), dtype=bool))
    diff = b[:, :, None] - b[:, None, :]
    decay = jnp.exp(jnp.where(causal[None, :, :, None, None], diff, -jnp.inf))
    scores = jnp.einsum('bihk,bjhk,bijhk->bhij', q, k, decay)
    o_intra = jnp.einsum('bhij,bjhv->bihv', scores, v)
    b_last = b[:, -1]
    S_new = jnp.exp(b_last)[..., None] * S + jnp.einsum(
        'bjhk,bjhv->bhkv', k * jnp.exp(b_last[:, None] - b), v)
    return S_new, o_inter + o_intra


def gla_scan(q, k, v, la, S0, chunk):
    B, L = q.shape[:2]
    n = L // chunk

    def to_chunks(t):
        return jnp.moveaxis(t.reshape(B, n, chunk, *t.shape[2:]), 1, 0)

    S_fin, o = lax.scan(lambda S, xs: gla_chunk(S, *xs), S0,
                        (to_chunks(q), to_chunks(k), to_chunks(v), to_chunks(la)))
    o = jnp.moveaxis(o, 0, 1).reshape(B, L, *o.shape[3:])
    return o, S_fin


def gla_branch(q, k, v, r, glr, S0, chunk, w_g2, b_g, g_norm, w_out):
    B, L, _ = q.shape
    f32 = jnp.float32

    def heads(t, d):
        return t.reshape(B, L, H_A, d).astype(f32)

    la = jax.nn.log_sigmoid((glr @ w_g2 + b_g).astype(f32)) / GATE_TAU
    o, S = gla_scan(heads(q, DK_HEAD_A) * DK_HEAD_A ** -0.5, heads(k, DK_HEAD_A),
                    heads(v, DV_HEAD_A), heads(la, DK_HEAD_A), S0.astype(f32), chunk)
    o = rms_norm(o, g_norm).reshape(B, L, DV_A).astype(q.dtype) * jax.nn.silu(r)
    return o @ w_out, S


def conv_branch(u2, buf, w_dw, b_dw, ln_g, ln_b, w_out):
    a, gte = jnp.split(u2, 2, axis=-1)
    u = a * jax.nn.sigmoid(gte)
    full = jnp.concatenate([buf.astype(u.dtype), u], axis=1)
    y = lax.conv_general_dilated(full, w_dw[:, None, :].astype(u.dtype), window_strides=(1,),
                                 padding='VALID', dimension_numbers=('NWC', 'WIO', 'NWC'),
                                 feature_group_count=C_CONV) + b_dw
    y = jax.nn.silu(layer_norm(y, ln_g, ln_b))
    return y @ w_out, full[:, -(CONV_K - 1):]


def fox_attend(q, k, v, cq, ck, q_pos, k_pos):
    s = jnp.einsum('bqhd,bshd->bhqs', q, k).astype(jnp.float32) * DH_F ** -0.5
    bias = jnp.moveaxis(cq, 2, 1)[..., :, None] - jnp.moveaxis(ck, 2, 1)[..., None, :]
    mask = k_pos[None, :] <= q_pos[:, None]
    p = jax.nn.softmax(jnp.where(mask, s + bias, -jnp.inf), axis=-1)
    return jnp.einsum('bhqs,bshd->bqhd', p.astype(v.dtype), v)


def fox_prompt(q, k, v, logf):
    B, L = q.shape[:2]
    c = jnp.cumsum(logf.astype(jnp.float32), axis=1)
    nb = L // Q_BLOCK
    qb = jnp.moveaxis(q.reshape(B, nb, Q_BLOCK, H_F, DH_F), 1, 0)
    cb = jnp.moveaxis(c.reshape(B, nb, Q_BLOCK, H_F), 1, 0)
    k_pos = jnp.arange(L)

    def one(args):
        i, qi, ci = args
        return fox_attend(qi, k, v, ci, c, i * Q_BLOCK + jnp.arange(Q_BLOCK), k_pos)

    o = lax.map(one, (jnp.arange(nb), qb, cb))
    return jnp.moveaxis(o, 0, 1).reshape(B, L, D_F)


def fox_sample(q, k, v, logf, pool_k, pool_v, pool_lf, page_table, layer):
    B, L = q.shape[:2]

    def gather(pool):
        g = pool[layer, page_table]
        return g.reshape(B, -1, *pool.shape[3:])

    k_all = jnp.concatenate([gather(pool_k).astype(k.dtype), k], axis=1)
    v_all = jnp.concatenate([gather(pool_v).astype(v.dtype), v], axis=1)
    c = jnp.cumsum(jnp.concatenate([gather(pool_lf).astype(jnp.float32),
                                    logf.astype(jnp.float32)], axis=1), axis=1)
    past = k_all.shape[1] - L
    o = fox_attend(q, k_all, v_all, c[:, past:], c, past + jnp.arange(L), jnp.arange(past + L))
    return o.reshape(B, L, D_F)


def token_mixing(h, S0, conv_buf, fox_fn, w_in, w_g2, b_g, gla_norm, w_gla_out, w_dw, b_dw,
                 ln_g, ln_b, w_conv_out, b_f, w_fox_out, w_merge_out):
    B, L, _ = h.shape
    qa, ka, va, ra, glr, u2, qf, kf, vf, fl, gl = split_in(h @ w_in)
    chunk = GLA_CHUNK if L % GLA_CHUNK == 0 else L
    y_a, S_new = gla_branch(qa, ka, va, ra, glr, S0, chunk, w_g2, b_g, gla_norm, w_gla_out)
    y_b, buf_new = conv_branch(u2, conv_buf, w_dw, b_dw, ln_g, ln_b, w_conv_out)
    qf = qf.reshape(B, L, H_F, DH_F)
    kf = kf.reshape(B, L, H_F, DH_F)
    vf = vf.reshape(B, L, H_F, DH_F)
    logf = jax.nn.log_sigmoid((fl + b_f).astype(jnp.float32))
    y_c = fox_fn(qf, kf, vf, logf) @ w_fox_out
    g_a, g_b, g_c = jnp.split(jax.nn.sigmoid(gl), N_BRANCH, axis=-1)
    merged = (g_a * y_a + g_b * y_b + g_c * y_c) @ w_merge_out
    return merged, S_new, buf_new, kf, vf, logf


def setup_inputs(seed: int = 0) -> dict:
    key = jax.random.key(seed)
    ks = iter(jax.random.split(key, 48))

    def nrm(shape, scale):
        return scale * jax.random.normal(next(ks), shape, jnp.float32)

    def gain(shape):
        return 1.0 + 0.1 * jax.random.normal(next(ks), shape, jnp.float32)

    n_pages = PAST_LEN // PAGE_SIZE
    n_pool = (DEC_BATCH * n_pages * 5) // 4
    page_table = jax.random.permutation(next(ks), n_pool)[:DEC_BATCH * n_pages].reshape(
        DEC_BATCH, n_pages).astype(jnp.int32)
    D = D_MODEL
    return {
        "x_prompt": nrm((BATCH, SEQ, D), 1.0),
        "x_sample": nrm((DEC_BATCH, DEC_SEQ, D), 1.0),
        "cache_k": nrm((DEPTH, n_pool, PAGE_SIZE, H_F, DH_F), 1.0),
        "cache_v": nrm((DEPTH, n_pool, PAGE_SIZE, H_F, DH_F), 1.0),
        "cache_logf": jax.nn.log_sigmoid(FORGET_BIAS + nrm((DEPTH, n_pool, PAGE_SIZE, H_F), 1.0)),
        "state_gla": nrm((DEPTH, DEC_BATCH, H_A, DK_HEAD_A, DV_HEAD_A), 1.0),
        "state_conv": nrm((DEPTH, DEC_BATCH, CONV_K - 1, C_CONV), 0.5),
        "page_table": page_table,
        "norm_ffn1_pre": gain((DEPTH, D)),
        "norm_ffn1_post": gain((DEPTH, D)),
        "w_ffn1_gate": nrm((DEPTH, D, D_FF), D ** -0.5),
        "w_ffn1_up": nrm((DEPTH, D, D_FF), D ** -0.5),
        "w_ffn1_down": nrm((DEPTH, D_FF, D), D_FF ** -0.5),
        "norm_mix_pre": gain((DEPTH, D)),
        "norm_mix_post": gain((DEPTH, D)),
        "w_in": nrm((DEPTH, D, D_IN), D ** -0.5),
        "w_gla_gate2": nrm((DEPTH, GATE_RANK, DK_A), GATE_RANK ** -0.5),
        "b_gla_gate": nrm((DEPTH, DK_A), 0.1),
        "gla_norm": gain((DEPTH, DV_HEAD_A)),
        "w_gla_out": nrm((DEPTH, DV_A, D), DV_A ** -0.5),
        "w_dw": nrm((DEPTH, CONV_K, C_CONV), CONV_K ** -0.5),
        "b_dw": nrm((DEPTH, C_CONV), 0.02),
        "conv_ln_g": gain((DEPTH, C_CONV)),
        "conv_ln_b": nrm((DEPTH, C_CONV), 0.02),
        "w_conv_out": nrm((DEPTH, C_CONV, D), C_CONV ** -0.5),
        "b_forget": FORGET_BIAS + nrm((DEPTH, H_F), 0.5),
        "w_fox_out": nrm((DEPTH, D_F, D), D_F ** -0.5),
        "w_merge_out": nrm((DEPTH, D, D), D ** -0.5),
        "norm_ffn2_pre": gain((DEPTH, D)),
        "norm_ffn2_post": gain((DEPTH, D)),
        "w_ffn2_gate": nrm((DEPTH, D, D_FF), D ** -0.5),
        "w_ffn2_up": nrm((DEPTH, D, D_FF), D ** -0.5),
        "w_ffn2_down": nrm((DEPTH, D_FF, D), D_FF ** -0.5),
    }


def reference(x_prompt, x_sample, cache_k, cache_v, cache_logf, state_gla, state_conv, page_table,
              norm_ffn1_pre, norm_ffn1_post, w_ffn1_gate, w_ffn1_up, w_ffn1_down,
              norm_mix_pre, norm_mix_post, w_in, w_gla_gate2, b_gla_gate, gla_norm, w_gla_out,
              w_dw, b_dw, conv_ln_g, conv_ln_b, w_conv_out, b_forget, w_fox_out, w_merge_out,
              norm_ffn2_pre, norm_ffn2_post, w_ffn2_gate, w_ffn2_up, w_ffn2_down):
    yp, ys = x_prompt, x_sample
    kp_l, vp_l, lfp_l, Sp_l, bp_l = [], [], [], [], []
    ks_l, vs_l, lfs_l, Ss_l, bs_l = [], [], [], [], []
    for l in range(DEPTH):
        ffn1 = (norm_ffn1_pre[l], norm_ffn1_post[l], w_ffn1_gate[l], w_ffn1_up[l], w_ffn1_down[l])
        ffn2 = (norm_ffn2_pre[l], norm_ffn2_post[l], w_ffn2_gate[l], w_ffn2_up[l], w_ffn2_down[l])
        mix_w = (w_in[l], w_gla_gate2[l], b_gla_gate[l], gla_norm[l], w_gla_out[l], w_dw[l], b_dw[l],
                 conv_ln_g[l], conv_ln_b[l], w_conv_out[l], b_forget[l], w_fox_out[l], w_merge_out[l])

        yp = half_ffn(yp, *ffn1)
        ys = half_ffn(ys, *ffn1)

        S0 = jnp.zeros((yp.shape[0], H_A, DK_HEAD_A, DV_HEAD_A), jnp.float32)
        buf0 = jnp.zeros((yp.shape[0], CONV_K - 1, C_CONV), yp.dtype)
        mp, Sp, bp, kp, vp, lfp = token_mixing(rms_norm(yp, norm_mix_pre[l]), S0, buf0,
                                               fox_prompt, *mix_w)
        fox_s = functools.partial(fox_sample, pool_k=cache_k, pool_v=cache_v, pool_lf=cache_logf,
                                  page_table=page_table, layer=l)
        ms, Ss, bs, kss, vss, lfs = token_mixing(rms_norm(ys, norm_mix_pre[l]), state_gla[l],
                                                 state_conv[l], fox_s, *mix_w)
        yp = yp + rms_norm(mp, norm_mix_post[l])
        ys = ys + rms_norm(ms, norm_mix_post[l])

        yp = half_ffn(yp, *ffn2)
        ys = half_ffn(ys, *ffn2)

        kp_l.append(kp); vp_l.append(vp); lfp_l.append(lfp); Sp_l.append(Sp); bp_l.append(bp)
        ks_l.append(kss); vs_l.append(vss); lfs_l.append(lfs); Ss_l.append(Ss); bs_l.append(bs)

    return (yp, ys,
            jnp.stack(kp_l), jnp.stack(vp_l), jnp.stack(lfp_l), jnp.stack(Sp_l), jnp.stack(bp_l),
            jnp.stack(ks_l), jnp.stack(vs_l), jnp.stack(lfs_l), jnp.stack(Ss_l), jnp.stack(bs_l))
```

```python
import functools
import math

import numpy as np
import jax
import jax.numpy as jnp
from jax import lax
from jax.experimental import pallas as pl
from jax.experimental.pallas import tpu as pltpu

F32 = jnp.float32
BF16 = jnp.bfloat16

D_MODEL = 2048
H_A = 4
DK_HEAD_A = 128
DV_HEAD_A = 256
DK_A = H_A * DK_HEAD_A
DV_A = H_A * DV_HEAD_A
GATE_RANK = 16
GATE_TAU = 16.0
C_CONV = 1024
CONV_K = 31
H_F = 8
DH_F = 128
D_F = H_F * DH_F
PAGE_SIZE = 128
D_FF = 5504
EPS = 1e-6

LANES = 128
V7X_VMEM_BYTES = 64 * 1024 * 1024
VMEM_LIMIT = 56 * 1024 * 1024

D_FF_PAD = 5632
FFN_TF = 512
GLA_CHUNK = 128
GLA_LEVELS = 7
CONV_TS = 256
CONV_HALO = 32
NEG_BIG = -1e30


def _cparams(sem, vmem=None):
    return pltpu.CompilerParams(dimension_semantics=sem, vmem_limit_bytes=vmem)


def _log_sigmoid(x):
    return jnp.minimum(x, 0.0) - jnp.log1p(jnp.exp(-jnp.abs(x)))


def _sigmoid(x):
    return 1.0 / (1.0 + jnp.exp(-x))


def _rms(x, g):
    return x * lax.rsqrt(jnp.mean(x * x, axis=-1, keepdims=True) + EPS) * g


def _split_dot(a, w, n_parts):
    out = None
    rem = a
    for i in range(n_parts):
        part = rem.astype(BF16)
        d = jnp.dot(part, w, preferred_element_type=F32)
        out = d if out is None else out + d
        if i + 1 < n_parts:
            rem = rem - part.astype(F32)
    return out


def _ffn_kernel(x_ref, gpre_ref, gpost_ref, gnext_ref, wg_ref, wu_ref, wd_ref,
                o_ref, hn_ref, h_scr, acc_scr):
    j = pl.program_id(1)

    @pl.when(j == 0)
    def _():
        h_scr[...] = _rms(x_ref[...], gpre_ref[...]).astype(BF16)
        acc_scr[...] = jnp.zeros_like(acc_scr)

    h = h_scr[...]
    g = jnp.dot(h, wg_ref[...], preferred_element_type=F32)
    u = jnp.dot(h, wu_ref[...], preferred_element_type=F32)
    a = (g * _sigmoid(g) * u).astype(BF16)
    acc_scr[...] += jnp.dot(a, wd_ref[...], preferred_element_type=F32)

    @pl.when(j == pl.num_programs(1) - 1)
    def _():
        xn = x_ref[...] + 0.5 * _rms(acc_scr[...], gpost_ref[...])
        o_ref[...] = xn
        hn_ref[...] = _rms(xn, gnext_ref[...]).astype(BF16)


def _ffn(x, g_pre, g_post, g_next, wg, wu, wd, tm):
    m, d = x.shape
    fp = wg.shape[1]
    tf = FFN_TF
    row = lambda i, j: (i, 0)
    vec = pl.BlockSpec((1, d), lambda i, j: (0, 0))
    return pl.pallas_call(
        _ffn_kernel,
        out_shape=(jax.ShapeDtypeStruct((m, d), F32), jax.ShapeDtypeStruct((m, d), BF16)),
        grid=(m // tm, fp // tf),
        in_specs=[pl.BlockSpec((tm, d), row), vec, vec, vec,
                  pl.BlockSpec((d, tf), lambda i, j: (0, j)),
                  pl.BlockSpec((d, tf), lambda i, j: (0, j)),
                  pl.BlockSpec((tf, d), lambda i, j: (j, 0))],
        out_specs=(pl.BlockSpec((tm, d), row), pl.BlockSpec((tm, d), row)),
        scratch_shapes=[pltpu.VMEM((tm, d), BF16), pltpu.VMEM((tm, d), F32)],
        compiler_params=_cparams(("parallel", "arbitrary"), VMEM_LIMIT),
        name="ffn_half",
    )(x, g_pre, g_post, g_next, wg, wu, wd)


def _proj_kernel(*refs, n_w, n_vec, epilogue):
    h = refs[0][...]
    accs = [jnp.dot(h, refs[1 + i][...], preferred_element_type=F32) for i in range(n_w)]
    vecs = [refs[1 + n_w + i][...] for i in range(n_vec)]
    outs = epilogue(*accs, *vecs)
    for o_ref, o in zip(refs[1 + n_w + n_vec:], outs):
        o_ref[...] = o.astype(o_ref.dtype)


def _proj(h, ws, vecs, epilogue, out_dtypes, tm, tn, name):
    m, k = h.shape
    n = ws[0].shape[1]
    kern = functools.partial(_proj_kernel, n_w=len(ws), n_vec=len(vecs), epilogue=epilogue)
    outs = pl.pallas_call(
        kern,
        out_shape=tuple(jax.ShapeDtypeStruct((m, n), dt) for dt in out_dtypes),
        grid=(m // tm, n // tn),
        in_specs=[pl.BlockSpec((tm, k), lambda i, j: (i, 0))]
        + [pl.BlockSpec((k, tn), lambda i, j: (0, j)) for _ in ws]
        + [pl.BlockSpec((1, tn), lambda i, j: (0, j)) for _ in vecs],
        out_specs=tuple(pl.BlockSpec((tm, tn), lambda i, j: (i, j)) for _ in out_dtypes),
        compiler_params=_cparams(("parallel", "parallel"), VMEM_LIMIT),
        name=name,
    )(h, *ws, *vecs)
    return outs


def _ep_id(a):
    return (a,)


def _ep_glu(a, g):
    return (a * _sigmoid(g),)


def _ep_sigmoid(a):
    return (_sigmoid(a),)


def _ep_fox(q, k, v):
    return (q * (DH_F ** -0.5), k, k, v, v)


def _ep_logsig_bias(a, b):
    return (_log_sigmoid(a + b),)


def _ep_gla_gate(a, b):
    return (_log_sigmoid(a + b) * (1.0 / GATE_TAU),)


def _gla_tables(lc):
    nlev = int(math.log2(lc))
    t = np.arange(lc)
    rows = [np.tril(np.ones((lc, lc))),
            np.triu(np.ones((lc, lc)), 1)]
    masks, upper = [], []
    for lev in range(nlev):
        s = lc >> lev
        mid = (t // s) * s + s // 2
        up = t >= mid
        u = t[None, :]
        c = np.where(up[:, None], (u >= mid[:, None]) & (u <= t[:, None]),
                     (u > t[:, None]) & (u < mid[:, None]))
        rows.append(c.astype(np.float64))
        masks.append(((t[:, None] // s) == (t[None, :] // s)).astype(np.float32))
        upper.append(up.astype(np.float32))
    masks.append(np.eye(lc, dtype=np.float32))
    cmat = np.concatenate(rows, axis=0).astype(np.float32)
    return (jnp.asarray(cmat, BF16), jnp.asarray(np.stack(masks)),
            jnp.asarray(np.stack(upper)[:, :, None]))


def _gla_prompt_kernel(q_ref, k_ref, v_ref, r_ref, glr_ref, wg2_ref, bg_ref, gn_ref,
                       cmat_ref, mask_ref, up_ref, o_ref, s_ref, *, lc, nlev):
    c = pl.program_id(2)

    @pl.when(c == 0)
    def _():
        s_ref[...] = jnp.zeros_like(s_ref)

    q = q_ref[...].astype(F32) * (DK_HEAD_A ** -0.5)
    k = k_ref[...].astype(F32)
    v = v_ref[...]
    x = jnp.dot(glr_ref[...], wg2_ref[...], preferred_element_type=F32) + bg_ref[...]
    la = _log_sigmoid(x) * (1.0 / GATE_TAU)
    g = _split_dot_left(cmat_ref[...], la)
    b = g[0:lc]
    rem = g[lc:2 * lc]

    nt = (((1,), (1,)), ((), ()))
    scores = lax.dot_general(q.astype(BF16), k.astype(BF16), nt,
                             preferred_element_type=F32) * mask_ref[nlev]
    for lev in range(nlev):
        e = jnp.exp(g[(2 + lev) * lc:(3 + lev) * lc])
        up = up_ref[lev]
        qe = (q * e * up).astype(BF16)
        ke = (k * e * (1.0 - up)).astype(BF16)
        scores = scores + lax.dot_general(qe, ke, nt, preferred_element_type=F32) * mask_ref[lev]

    s_old = s_ref[...]
    o = jnp.dot((q * jnp.exp(b)).astype(BF16), s_old.astype(BF16), preferred_element_type=F32)
    o = o + jnp.dot(scores.astype(BF16), v, preferred_element_type=F32)

    kd_t = (k * jnp.exp(rem)).T.astype(BF16)
    decay_col = jnp.exp(jnp.sum(la.T, axis=1, keepdims=True))
    s_ref[...] = decay_col * s_old + jnp.dot(kd_t, v, preferred_element_type=F32)

    r = r_ref[...].astype(F32)
    o_ref[...] = (_rms(o, gn_ref[...]) * (r * _sigmoid(r))).astype(o_ref.dtype)


def _split_dot_left(w, a):
    hi = a.astype(BF16)
    lo = (a - hi.astype(F32)).astype(BF16)
    return (jnp.dot(w, hi, preferred_element_type=F32)
            + jnp.dot(w, lo, preferred_element_type=F32))


def _gla_prompt(za, wg2p, bg, gn, batch, seq):
    lc, nlev = GLA_CHUNK, GLA_LEVELS
    nc = seq // lc
    cmat, masks, upper = _gla_tables(lc)
    rowi = lambda b, h, c: b * nc + c
    kern = functools.partial(_gla_prompt_kernel, lc=lc, nlev=nlev)
    const3 = lambda b, h, c: (0, 0, 0)
    return pl.pallas_call(
        kern,
        out_shape=(jax.ShapeDtypeStruct((batch * seq, DV_A), BF16),
                   jax.ShapeDtypeStruct((batch, H_A, DK_HEAD_A, DV_HEAD_A), F32)),
        grid=(batch, H_A, nc),
        in_specs=[
            pl.BlockSpec((lc, DK_HEAD_A), lambda b, h, c: (rowi(b, h, c), h)),
            pl.BlockSpec((lc, DK_HEAD_A), lambda b, h, c: (rowi(b, h, c), H_A + h)),
            pl.BlockSpec((lc, DV_HEAD_A), lambda b, h, c: (rowi(b, h, c), H_A + h)),
            pl.BlockSpec((lc, DV_HEAD_A), lambda b, h, c: (rowi(b, h, c), 2 * H_A + h)),
            pl.BlockSpec((lc, LANES), lambda b, h, c: (rowi(b, h, c), (2 * DK_A + 2 * DV_A) // LANES)),
            pl.BlockSpec((LANES, DK_HEAD_A), lambda b, h, c: (0, h)),
            pl.BlockSpec((1, DK_HEAD_A), lambda b, h, c: (0, h)),
            pl.BlockSpec((1, DV_HEAD_A), lambda b, h, c: (0, 0)),
            pl.BlockSpec(cmat.shape, lambda b, h, c: (0, 0)),
            pl.BlockSpec(masks.shape, const3),
            pl.BlockSpec(upper.shape, const3),
        ],
        out_specs=(pl.BlockSpec((lc, DV_HEAD_A), lambda b, h, c: (rowi(b, h, c), h)),
                   pl.BlockSpec((None, None, DK_HEAD_A, DV_HEAD_A), lambda b, h, c: (b, h, 0, 0))),
        compiler_params=_cparams(("parallel", "parallel", "arbitrary")),
        name="gla_prompt",
    )(za, za, za, za, za, wg2p, bg, gn, cmat, masks, upper)


def _gla_sample_kernel(qc_ref, kc_ref, lac_ref, v_ref, r_ref, gn_ref, s_ref, o_ref, so_ref):
    s_new = jnp.exp(lac_ref[...]) * s_ref[...] + kc_ref[...] * v_ref[...].astype(F32)
    so_ref[...] = s_new
    o = jnp.sum((qc_ref[...] * (DK_HEAD_A ** -0.5)) * s_new, axis=0, keepdims=True)
    r = r_ref[...].astype(F32)
    o_ref[...] = (_rms(o, gn_ref[...]) * (r * _sigmoid(r))).astype(o_ref.dtype)


def _gla_sample(za3, q_col, k_col, la_col, gn, state):
    nb = za3.shape[0]
    col = pl.BlockSpec((None, None, DK_HEAD_A, 1), lambda b, h: (b, h, 0, 0))
    st = pl.BlockSpec((None, None, DK_HEAD_A, DV_HEAD_A), lambda b, h: (b, h, 0, 0))
    return pl.pallas_call(
        _gla_sample_kernel,
        out_shape=(jax.ShapeDtypeStruct((nb, 1, DV_A), BF16),
                   jax.ShapeDtypeStruct(state.shape, F32)),
        grid=(nb, H_A),
        in_specs=[col, col, col,
                  pl.BlockSpec((None, 1, DV_HEAD_A), lambda b, h: (b, 0, H_A + h)),
                  pl.BlockSpec((None, 1, DV_HEAD_A), lambda b, h: (b, 0, 2 * H_A + h)),
                  pl.BlockSpec((1, DV_HEAD_A), lambda b, h: (0, 0)),
                  st],
        out_specs=(pl.BlockSpec((None, 1, DV_HEAD_A), lambda b, h: (b, 0, h)), st),
        compiler_params=_cparams(("parallel", "parallel")),
        name="gla_sample",
    )(q_col, k_col, la_col, za3, za3, gn, state)


def _ln_swish(y, g, b):
    mu = jnp.mean(y, axis=-1, keepdims=True)
    yc = y - mu
    var = jnp.mean(yc * yc, axis=-1, keepdims=True)
    z = yc * lax.rsqrt(var + EPS) * g + b
    return z * _sigmoid(z)


def _conv_prompt_kernel(prev_ref, cur_ref, w_ref, bdw_ref, lng_ref, lnb_ref, o_ref, buf, ybuf, *, ts):
    t = pl.program_id(1)
    keep = (t > 0).astype(F32)
    buf[0:CONV_HALO, :] = prev_ref[...] * keep
    buf[CONV_HALO:, :] = cur_ref[...]
    first = CONV_HALO - (CONV_K - 1)
    rb = 64
    for cb in range(C_CONV // LANES):
        cs = slice(cb * LANES, (cb + 1) * LANES)
        for r0 in range(0, ts, rb):
            acc = jnp.zeros((rb, LANES), F32)
            for j in range(CONV_K):
                acc = acc + w_ref[j:j + 1, cs] * buf[r0 + first + j:r0 + first + j + rb, cs]
            ybuf[r0:r0 + rb, cs] = acc
    y = ybuf[...] + bdw_ref[...]
    o_ref[...] = _ln_swish(y, lng_ref[...], lnb_ref[...]).astype(o_ref.dtype)


def _conv_prompt(u, w_dw, b_dw, ln_g, ln_b, batch, seq):
    ts = CONV_TS
    nt = seq // ts
    per = ts // CONV_HALO
    vec = pl.BlockSpec((1, C_CONV), lambda b, t: (0, 0))
    kern = functools.partial(_conv_prompt_kernel, ts=ts)
    return pl.pallas_call(
        kern,
        out_shape=jax.ShapeDtypeStruct((batch * seq, C_CONV), BF16),
        grid=(batch, nt),
        in_specs=[pl.BlockSpec((CONV_HALO, C_CONV),
                               lambda b, t: (jnp.maximum((b * nt + t) * per - 1, 0), 0)),
                  pl.BlockSpec((ts, C_CONV), lambda b, t: (b * nt + t, 0)),
                  pl.BlockSpec((CONV_K, C_CONV), lambda b, t: (0, 0)),
                  vec, vec, vec],
        out_specs=pl.BlockSpec((ts, C_CONV), lambda b, t: (b * nt + t, 0)),
        scratch_shapes=[pltpu.VMEM((ts + CONV_HALO, C_CONV), F32), pltpu.VMEM((ts, C_CONV), F32)],
        compiler_params=_cparams(("parallel", "parallel")),
        name="conv_prompt",
    )(u, u, w_dw, b_dw, ln_g, ln_b)


def _conv_sample_kernel(buf_ref, u_ref, w_ref, bdw_ref, lng_ref, lnb_ref, o_ref, nb_ref):
    buf = buf_ref[...]
    u = u_ref[...]
    y = (jnp.sum(buf * w_ref[0:CONV_K - 1, :], axis=0, keepdims=True)
         + u * w_ref[CONV_K - 1:CONV_K, :] + bdw_ref[...])
    o_ref[...] = _ln_swish(y, lng_ref[...], lnb_ref[...]).astype(o_ref.dtype)
    nb_ref[0:CONV_K - 2, :] = buf[1:CONV_K - 1, :]
    nb_ref[CONV_K - 2:CONV_K - 1, :] = u


def _conv_sample(state, u3, w_dw, b_dw, ln_g, ln_b):
    nb = state.shape[0]
    vec = pl.BlockSpec((1, C_CONV), lambda b: (0, 0))
    st = pl.BlockSpec((None, CONV_K - 1, C_CONV), lambda b: (b, 0, 0))
    row = pl.BlockSpec((None, 1, C_CONV), lambda b: (b, 0, 0))
    return pl.pallas_call(
        _conv_sample_kernel,
        out_shape=(jax.ShapeDtypeStruct((nb, 1, C_CONV), BF16),
                   jax.ShapeDtypeStruct(state.shape, F32)),
        grid=(nb,),
        in_specs=[st, row, pl.BlockSpec((CONV_K, C_CONV), lambda b: (0, 0)), vec, vec, vec],
        out_specs=(row, st),
        compiler_params=_cparams(("parallel",)),
        name="conv_sample",
    )(state, u3, w_dw, b_dw, ln_g, ln_b)


def _fox_cum_kernel(h_ref, wt_ref, bcol_ref, c_ref):
    nt = (((1,), (1,)), ((), ()))
    fl = lax.dot_general(wt_ref[...], h_ref[...], nt, preferred_element_type=F32)
    x = _log_sigmoid(fl + bcol_ref[...])
    n = x.shape[1]
    lane = lax.broadcasted_iota(jnp.int32, x.shape, 1)
    sh = 1
    while sh < n:
        x = x + jnp.where(lane >= sh, pltpu.roll(x, sh, axis=1), 0.0)
        sh *= 2
    c_ref[...] = x[0:H_F]


def _fox_cum(h, w_fl_t, b_col, batch, seq):
    d = h.shape[1]
    return pl.pallas_call(
        _fox_cum_kernel,
        out_shape=jax.ShapeDtypeStruct((batch, H_F, seq), F32),
        grid=(batch,),
        in_specs=[pl.BlockSpec((seq, d), lambda b: (b, 0)),
                  pl.BlockSpec(w_fl_t.shape, lambda b: (0, 0)),
                  pl.BlockSpec(b_col.shape, lambda b: (0, 0))],
        out_specs=pl.BlockSpec((None, H_F, seq), lambda b: (b, 0, 0)),
        compiler_params=_cparams(("parallel",), VMEM_LIMIT),
        name="fox_cum",
    )(h, w_fl_t, b_col)


def _fox_prompt_kernel(q_ref, k_ref, v_ref, c_ref, o_ref, m_scr, l_scr, acc_scr, *, tq, tk):
    qi = pl.program_id(1)
    ki = pl.program_id(2)

    @pl.when(ki == 0)
    def _():
        m_scr[...] = jnp.full_like(m_scr, NEG_BIG)
        l_scr[...] = jnp.zeros_like(l_scr)
        acc_scr[...] = jnp.zeros_like(acc_scr)

    @pl.when(ki <= qi)
    def _():
        nt = (((1,), (1,)), ((), ()))
        row = lax.broadcasted_iota(jnp.int32, (tq, tk), 0)
        col = lax.broadcasted_iota(jnp.int32, (tq, tk), 1)
        visible = (ki < qi) | (col <= row)
        for h in range(H_F):
            hs = slice(h * DH_F, (h + 1) * DH_F)
            s = lax.dot_general(q_ref[:, hs], k_ref[:, hs], nt, preferred_element_type=F32)
            s = jnp.where(visible, s - c_ref[h:h + 1, :], NEG_BIG)
            m_prev = m_scr[h]
            m_new = jnp.maximum(m_prev, jnp.max(s, axis=1, keepdims=True))
            alpha = jnp.exp(m_prev - m_new)
            p = jnp.exp(s - jnp.concatenate([m_new] * (tk // LANES), axis=1))
            l_scr[h] = alpha * l_scr[h] + jnp.sum(p, axis=1, keepdims=True)
            m_scr[h] = m_new
            acc_scr[:, hs] = alpha * acc_scr[:, hs] + jnp.dot(
                p.astype(BF16), v_ref[:, hs], preferred_element_type=F32)

    @pl.when(ki == qi)
    def _():
        for h in range(H_F):
            hs = slice(h * DH_F, (h + 1) * DH_F)
            o_ref[:, hs] = (acc_scr[:, hs] / l_scr[h]).astype(o_ref.dtype)


def _fox_prompt(q, k, v, c_t, batch, seq, tq=256, tk=256):
    nq, nk = seq // tq, seq // tk
    kern = functools.partial(_fox_prompt_kernel, tq=tq, tk=tk)
    kv = pl.BlockSpec((tk, D_F), lambda b, i, j: (b * nk + jnp.minimum(j, i), 0))
    return pl.pallas_call(
        kern,
        out_shape=jax.ShapeDtypeStruct((batch * seq, D_F), BF16),
        grid=(batch, nq, nk),
        in_specs=[pl.BlockSpec((tq, D_F), lambda b, i, j: (b * nq + i, 0)), kv, kv,
                  pl.BlockSpec((None, H_F, tk), lambda b, i, j: (b, 0, jnp.minimum(j, i)))],
        out_specs=pl.BlockSpec((tq, D_F), lambda b, i, j: (b * nq + i, 0)),
        scratch_shapes=[pltpu.VMEM((H_F, tq, LANES), F32), pltpu.VMEM((H_F, tq, LANES), F32),
                        pltpu.VMEM((tq, D_F), F32)],
        compiler_params=_cparams(("parallel", "parallel", "arbitrary")),
        name="fox_prompt",
    )(q, k, v, c_t)


def _fox_bias_kernel(pt_ref, lf_hbm, lfnew_ref, o_ref, buf, wsuf, wtot, sem, *, layer_off, n_pages):
    b = pl.program_id(0)

    @pl.when(b == 0)
    def _():
        f = lax.broadcasted_iota(jnp.int32, wsuf.shape, 0)
        c = lax.broadcasted_iota(jnp.int32, wsuf.shape, 1)
        same_head = (f % H_F) == (c % H_F)
        wtot[...] = jnp.where(same_head, 1.0, 0.0).astype(BF16)
        wsuf[...] = jnp.where(same_head & ((f // H_F) > (c // H_F)), 1.0, 0.0).astype(BF16)

    def page_copy(p):
        return pltpu.make_async_copy(lf_hbm.at[pl.ds(layer_off + pt_ref[b, p], 1)],
                                     buf.at[pl.ds(p, 1)], sem)

    for p in range(n_pages):
        page_copy(p).start()
    for p in range(n_pages):
        page_copy(p).wait()

    pages = buf[...]
    within = _split_dot(pages, wsuf[...], 3)
    tot = _split_dot(pages, wtot[...], 3)
    i0 = lax.broadcasted_iota(jnp.int32, (n_pages, n_pages), 0)
    i1 = lax.broadcasted_iota(jnp.int32, (n_pages, n_pages), 1)
    later = jnp.where(i1 > i0, 1.0, 0.0).astype(BF16)
    o_ref[...] = within + _split_dot_left3(later, tot) + lfnew_ref[...]


def _split_dot_left3(w, a):
    out = None
    rem = a
    for i in range(3):
        part = rem.astype(BF16)
        d = jnp.dot(w, part, preferred_element_type=F32)
        out = d if out is None else out + d
        rem = rem - part.astype(F32)
    return out


def _fox_bias(page_table, lf_flat, lfnew_rep, layer, n_pool):
    nb, n_pages = page_table.shape
    width = PAGE_SIZE * H_F
    kern = functools.partial(_fox_bias_kernel, layer_off=layer * n_pool, n_pages=n_pages)
    grid_spec = pltpu.PrefetchScalarGridSpec(
        num_scalar_prefetch=1,
        grid=(nb,),
        in_specs=[pl.BlockSpec(memory_space=pl.ANY),
                  pl.BlockSpec((None, 1, width), lambda b, pt: (b, 0, 0))],
        out_specs=pl.BlockSpec((None, n_pages, width), lambda b, pt: (b, 0, 0)),
        scratch_shapes=[pltpu.VMEM((n_pages, width), F32),
                        pltpu.VMEM((width, width), BF16), pltpu.VMEM((width, width), BF16),
                        pltpu.SemaphoreType.DMA(())],
    )
    return pl.pallas_call(
        kern,
        out_shape=jax.ShapeDtypeStruct((nb, n_pages, width), F32),
        grid_spec=grid_spec,
        compiler_params=_cparams(("arbitrary",)),
        name="fox_bias",
    )(page_table, lf_flat, lfnew_rep)


def _fox_sample_kernel(pt_ref, q_ref, kn_ref, vn_ref, k_ref, v_ref, bias_ref, o_ref,
                       m_scr, l_scr, acc_scr):
    p = pl.program_id(1)
    q = q_ref[...]

    @pl.when(p == 0)
    def _():
        s_new = jnp.sum(q.astype(F32) * kn_ref[...], axis=1, keepdims=True)
        m_scr[...] = jnp.broadcast_to(s_new, m_scr.shape)
        l_scr[...] = jnp.ones_like(l_scr)
        acc_scr[...] = vn_ref[...]

    rows = PAGE_SIZE * H_F
    nt = (((1,), (1,)), ((), ()))
    kf = k_ref[...].reshape(rows, DH_F).astype(BF16)
    vf = v_ref[...].reshape(rows, DH_F).astype(BF16)
    s = lax.dot_general(q, kf, nt, preferred_element_type=F32)
    own = (lax.broadcasted_iota(jnp.int32, (H_F, rows), 1) % H_F
           == lax.broadcasted_iota(jnp.int32, (H_F, rows), 0))
    s = jnp.where(own, s + bias_ref[...], NEG_BIG)
    m_prev = m_scr[...]
    m_new = jnp.maximum(m_prev, jnp.max(s, axis=1, keepdims=True))
    alpha = jnp.exp(m_prev - m_new)
    pr = jnp.exp(s - m_new[:, 0:1])
    l_scr[...] = alpha * l_scr[...] + jnp.sum(pr, axis=1, keepdims=True)
    m_scr[...] = m_new
    acc_scr[...] = alpha * acc_scr[...] + jnp.dot(pr.astype(BF16), vf, preferred_element_type=F32)

    @pl.when(p == pl.num_programs(1) - 1)
    def _():
        o_ref[...] = (acc_scr[...] / l_scr[...]).astype(o_ref.dtype)


def _fox_sample(page_table, q3, kn3, vn3, cache_k, cache_v, bias4, layer):
    nb, n_pages = page_table.shape
    row = pl.BlockSpec((None, H_F, DH_F), lambda b, p, pt: (b, 0, 0))
    page = pl.BlockSpec((None, None, PAGE_SIZE, H_F, DH_F),
                        lambda b, p, pt: (layer, pt[b, p], 0, 0, 0))
    grid_spec = pltpu.PrefetchScalarGridSpec(
        num_scalar_prefetch=1,
        grid=(nb, n_pages),
        in_specs=[row, row, row, page, page,
                  pl.BlockSpec((None, None, 1, PAGE_SIZE * H_F), lambda b, p, pt: (b, p, 0, 0))],
        out_specs=row,
        scratch_shapes=[pltpu.VMEM((H_F, LANES), F32), pltpu.VMEM((H_F, LANES), F32),
                        pltpu.VMEM((H_F, DH_F), F32)],
    )
    return pl.pallas_call(
        _fox_sample_kernel,
        out_shape=jax.ShapeDtypeStruct((nb, H_F, DH_F), BF16),
        grid_spec=grid_spec,
        compiler_params=_cparams(("parallel", "arbitrary")),
        name="fox_sample",
    )(page_table, q3, kn3, vn3, cache_k, cache_v, bias4)


def _merge_kernel(x_ref, oa_ref, ob_ref, oc_ref, ga_ref, gb_ref, gc_ref,
                  wa_ref, wb_ref, wc_ref, wm_ref, gpost_ref, o_ref):
    m = ga_ref[...].astype(F32) * jnp.dot(oa_ref[...], wa_ref[...], preferred_element_type=F32)
    m = m + gb_ref[...].astype(F32) * jnp.dot(ob_ref[...], wb_ref[...], preferred_element_type=F32)
    m = m + gc_ref[...].astype(F32) * jnp.dot(oc_ref[...], wc_ref[...], preferred_element_type=F32)
    y = jnp.dot(m.astype(BF16), wm_ref[...], preferred_element_type=F32)
    o_ref[...] = x_ref[...] + _rms(y, gpost_ref[...])


def _merge(x, oa, ob, oc, gates, wa, wb, wc, wm, g_post, tm):
    m, d = x.shape
    row = lambda i: (i, 0)
    const = lambda i: (0, 0)
    once = pl.Buffered(1)
    return pl.pallas_call(
        _merge_kernel,
        out_shape=jax.ShapeDtypeStruct((m, d), F32),
        grid=(m // tm,),
        in_specs=[pl.BlockSpec((tm, d), row),
                  pl.BlockSpec((tm, DV_A), row), pl.BlockSpec((tm, C_CONV), row),
                  pl.BlockSpec((tm, D_F), row),
                  pl.BlockSpec((tm, d), lambda i: (i, 0)), pl.BlockSpec((tm, d), lambda i: (i, 1)),
                  pl.BlockSpec((tm, d), lambda i: (i, 2)),
                  pl.BlockSpec(wa.shape, const, pipeline_mode=once),
                  pl.BlockSpec(wb.shape, const, pipeline_mode=once),
                  pl.BlockSpec(wc.shape, const, pipeline_mode=once),
                  pl.BlockSpec(wm.shape, const, pipeline_mode=once),
                  pl.BlockSpec((1, d), const)],
        out_specs=pl.BlockSpec((tm, d), row),
        compiler_params=_cparams(("parallel",), VMEM_LIMIT),
        name="merge",
    )(x, oa, ob, oc, gates, gates, gates, wa, wb, wc, wm, g_post)


def _pad_cols(w, n):
    return jnp.pad(w, ((0, 0), (0, n - w.shape[1])))


def _pad_rows(w, n):
    return jnp.pad(w, ((0, n - w.shape[0]), (0, 0)))


def _ffn_weights(wg, wu, wd):
    return (_pad_cols(wg, D_FF_PAD).astype(BF16), _pad_cols(wu, D_FF_PAD).astype(BF16),
            _pad_rows(wd, D_FF_PAD).astype(BF16))


def kernel(x_prompt, x_sample, cache_k, cache_v, cache_logf, state_gla, state_conv, page_table,
           norm_ffn1_pre, norm_ffn1_post, w_ffn1_gate, w_ffn1_up, w_ffn1_down,
           norm_mix_pre, norm_mix_post, w_in, w_gla_gate2, b_gla_gate, gla_norm, w_gla_out,
           w_dw, b_dw, conv_ln_g, conv_ln_b, w_conv_out, b_forget, w_fox_out, w_merge_out,
           norm_ffn2_pre, norm_ffn2_post, w_ffn2_gate, w_ffn2_up, w_ffn2_down):
    batch, seq, d = x_prompt.shape
    nb = x_sample.shape[0]
    depth = w_in.shape[0]
    n_pool = cache_k.shape[1]
    n_pages = page_table.shape[1]
    tm_p = min(512, batch * seq)
    tm_proj = min(1024, batch * seq)

    xp = x_prompt.reshape(batch * seq, d)
    xs = x_sample.reshape(nb, d)
    lf_flat = cache_logf.reshape(depth * n_pool, PAGE_SIZE * H_F)

    o_a = 2 * DK_A + 2 * DV_A + GATE_RANK
    o_u = o_a + 2 * C_CONV
    o_q, o_k, o_v = o_u + D_F, o_u + 2 * D_F, o_u + 3 * D_F
    o_f = o_v + H_F
    za_w = 2 * DK_A + 2 * DV_A + LANES

    outs = {n: [] for n in ("kp", "vp", "lfp", "sp", "bp", "ks", "vs", "lfs", "ss", "bs")}
    for l in range(depth):
        row = lambda a: a[l].reshape(1, -1)
        ffn1 = _ffn_weights(w_ffn1_gate[l], w_ffn1_up[l], w_ffn1_down[l])
        ffn2 = _ffn_weights(w_ffn2_gate[l], w_ffn2_up[l], w_ffn2_down[l])
        wi = w_in[l]
        w_za = _pad_cols(wi[:, :o_a], za_w).astype(BF16)
        w_ca = wi[:, o_a:o_a + C_CONV].astype(BF16)
        w_cg = wi[:, o_a + C_CONV:o_u].astype(BF16)
        w_q, w_k, w_v = (wi[:, o_u:o_q].astype(BF16), wi[:, o_q:o_k].astype(BF16),
                         wi[:, o_k:o_v].astype(BF16))
        w_fl = _pad_cols(wi[:, o_v:o_f], LANES).astype(BF16)
        w_fl_t = _pad_rows(wi[:, o_v:o_f].T, 16).astype(BF16)
        w_gl = wi[:, o_f:].astype(BF16)
        wg2p = _pad_rows(w_gla_gate2[l], LANES).astype(BF16)
        bg = row(b_gla_gate)
        gn = row(gla_norm)
        bf_row = _pad_cols(row(b_forget), LANES)
        bf_col = _pad_rows(b_forget[l].reshape(-1, 1), 16)
        w_ao, w_bo, w_co, w_mo = (w_gla_out[l].astype(BF16), w_conv_out[l].astype(BF16),
                                  w_fox_out[l].astype(BF16), w_merge_out[l].astype(BF16))
        next_g = row(norm_ffn1_pre) if l + 1 == depth else norm_ffn1_pre[l + 1].reshape(1, -1)

        def project(h, tm):
            (za,) = _proj(h, [w_za], [], _ep_id, [BF16], tm, 640, "proj_gla")
            (u,) = _proj(h, [w_ca, w_cg], [], _ep_glu, [F32], tm, 512, "proj_conv")
            q, k32, k16, v32, v16 = _proj(h, [w_q, w_k, w_v], [], _ep_fox,
                                          [BF16, F32, BF16, F32, BF16], tm, 512, "proj_fox")
            (lf,) = _proj(h, [w_fl], [bf_row], _ep_logsig_bias, [F32], tm, LANES, "proj_logf")
            (gates,) = _proj(h, [w_gl], [], _ep_sigmoid, [BF16], tm, 1024, "proj_gates")
            return za, u, q, k32, k16, v32, v16, lf[:, :H_F], gates

        xp, hp = _ffn(xp, row(norm_ffn1_pre), row(norm_ffn1_post), row(norm_mix_pre), *ffn1, tm_p)
        xs, hs = _ffn(xs, row(norm_ffn1_pre), row(norm_ffn1_post), row(norm_mix_pre), *ffn1, nb)

        za, u, q, k32, k16, v32, v16, lf, gates = project(hp, tm_proj)
        oa, s_fin = _gla_prompt(za, wg2p, bg, gn, batch, seq)
        ob = _conv_prompt(u, w_dw[l], row(b_dw), row(conv_ln_g), row(conv_ln_b), batch, seq)
        c_t = _fox_cum(hp, w_fl_t, bf_col, batch, seq)
        oc = _fox_prompt(q, k16, v16, c_t, batch, seq)
        xp = _merge(xp, oa, ob, oc, gates, w_ao, w_bo, w_co, w_mo, row(norm_mix_post), min(256, batch * seq))
        outs["kp"].append(k32.reshape(batch, seq, H_F, DH_F))
        outs["vp"].append(v32.reshape(batch, seq, H_F, DH_F))
        outs["lfp"].append(lf.reshape(batch, seq, H_F))
        outs["sp"].append(s_fin)
        outs["bp"].append(u.reshape(batch, seq, C_CONV)[:, seq - (CONV_K - 1):])

        za, u, q, k32, k16, v32, v16, lf, gates = project(hs, nb)
        (la,) = _proj(za[:, 2 * DK_A + 2 * DV_A:], [wg2p], [bg], _ep_gla_gate, [F32], nb, DK_A,
                      "gla_gate_sample")
        col = lambda a: a.astype(F32).reshape(nb, H_A, DK_HEAD_A, 1)
        oa, s_new = _gla_sample(za.reshape(nb, 1, za_w), col(za[:, :DK_A]),
                                col(za[:, DK_A:2 * DK_A]), col(la), gn, state_gla[l])
        ob, buf_new = _conv_sample(state_conv[l], u.reshape(nb, 1, C_CONV), w_dw[l], row(b_dw),
                                   row(conv_ln_g), row(conv_ln_b))
        lfnew_rep = jnp.tile(lf, (1, PAGE_SIZE)).reshape(nb, 1, PAGE_SIZE * H_F)
        bias = _fox_bias(page_table, lf_flat, lfnew_rep, l, n_pool)
        oc = _fox_sample(page_table, q.reshape(nb, H_F, DH_F), k32.reshape(nb, H_F, DH_F),
                         v32.reshape(nb, H_F, DH_F), cache_k, cache_v,
                         bias.reshape(nb, n_pages, 1, PAGE_SIZE * H_F), l)
        xs = _merge(xs, oa.reshape(nb, DV_A), ob.reshape(nb, C_CONV), oc.reshape(nb, D_F), gates,
                    w_ao, w_bo, w_co, w_mo, row(norm_mix_post), nb)
        outs["ks"].append(k32.reshape(nb, 1, H_F, DH_F))
        outs["vs"].append(v32.reshape(nb, 1, H_F, DH_F))
        outs["lfs"].append(lf.reshape(nb, 1, H_F))
        outs["ss"].append(s_new)
        outs["bs"].append(buf_new)

        xp, _ = _ffn(xp, row(norm_ffn2_pre), row(norm_ffn2_post), next_g, *ffn2, tm_p)
        xs, _ = _ffn(xs, row(norm_ffn2_pre), row(norm_ffn2_post), next_g, *ffn2, nb)

    st = lambda n: jnp.stack(outs[n])
    return (xp.reshape(batch, seq, d), xs.reshape(nb, 1, d),
            st("kp"), st("vp"), st("lfp"), st("sp"), st("bp"),
            st("ks"), st("vs"), st("lfs"), st("ss"), st("bs"))
```

```python
import functools
import math

import numpy as np
import jax
import jax.numpy as jnp
from jax import lax
from jax.experimental import pallas as pl
from jax.experimental.pallas import tpu as pltpu

F32 = jnp.float32
BF16 = jnp.bfloat16

H_A = 4
DK_HEAD_A = 128
DV_HEAD_A = 256
DK_A = H_A * DK_HEAD_A
DV_A = H_A * DV_HEAD_A
GATE_RANK = 16
GATE_TAU = 16.0
C_CONV = 1024
CONV_K = 31
H_F = 8
DH_F = 128
D_F = H_F * DH_F
PAGE_SIZE = 128
EPS = 1e-6

LANES = 128
V7X_VMEM_BYTES = 64 * 1024 * 1024
VMEM_LIMIT = V7X_VMEM_BYTES - 8 * 1024 * 1024

FFN_TF = 512
PROJ_TN = 512
GLA_CHUNK = 128
GLA_LEVELS = 7
CONV_TS = 256
CONV_HALO = 32
FOX_PAGES_PER_STEP = 8
BIAS_BATCH_PER_STEP = 4
NEG_BIG = -1e30


def _cparams(sem, vmem=None):
    return pltpu.CompilerParams(dimension_semantics=sem, vmem_limit_bytes=vmem)


def _log_sigmoid(x):
    return jnp.minimum(x, 0.0) - jnp.log1p(jnp.exp(-jnp.abs(x)))


def _sigmoid(x):
    return 1.0 / (1.0 + jnp.exp(-x))


def _rms(x, g):
    return x * lax.rsqrt(jnp.mean(x * x, axis=-1, keepdims=True) + EPS) * g


def _bf16_pieces(a, n):
    parts, rem = [], a
    for i in range(n):
        p = rem.astype(BF16)
        parts.append(p)
        if i + 1 < n:
            rem = rem - p.astype(F32)
    return parts


def _dot_exact_rhs(a, w, n):
    return sum(jnp.dot(p, w, preferred_element_type=F32) for p in _bf16_pieces(a, n))


def _dot_exact_lhs(w, a, n):
    return sum(jnp.dot(w, p, preferred_element_type=F32) for p in _bf16_pieces(a, n))


def _ffn_kernel(*refs, d_ff, with_next):
    if with_next:
        (x_ref, gpre_ref, gpost_ref, gnext_ref, wg_ref, wu_ref, wd_ref,
         o_ref, hn_ref, h_scr, acc_scr) = refs
    else:
        x_ref, gpre_ref, gpost_ref, wg_ref, wu_ref, wd_ref, o_ref, h_scr, acc_scr = refs
    j = pl.program_id(1)
    tf = wg_ref.shape[1]

    @pl.when(j == 0)
    def _():
        h_scr[...] = _rms(x_ref[...], gpre_ref[...]).astype(BF16)
        acc_scr[...] = jnp.zeros_like(acc_scr)

    n_valid = d_ff % tf

    def accumulate(ragged):
        h = h_scr[...]
        g = jnp.dot(h, wg_ref[...], preferred_element_type=F32)
        u = jnp.dot(h, wu_ref[...], preferred_element_type=F32)
        a = g * _sigmoid(g) * u
        wd = wd_ref[...]
        if ragged:
            a = jnp.where(lax.broadcasted_iota(jnp.int32, (1, tf), 1) < n_valid, a, 0.0)
            wd = jnp.where(lax.broadcasted_iota(jnp.int32, wd.shape, 0) < n_valid, wd,
                           jnp.zeros_like(wd))
        acc_scr[...] += jnp.dot(a.astype(BF16), wd, preferred_element_type=F32)

    if n_valid == 0:
        accumulate(False)
    else:
        pl.when(j < pl.num_programs(1) - 1)(lambda: accumulate(False))
        pl.when(j == pl.num_programs(1) - 1)(lambda: accumulate(True))

    @pl.when(j == pl.num_programs(1) - 1)
    def _():
        xn = x_ref[...] + 0.5 * _rms(acc_scr[...], gpost_ref[...])
        o_ref[...] = xn
        if with_next:
            hn_ref[...] = _rms(xn, gnext_ref[...]).astype(BF16)


def _ffn(x, g_pre, g_post, g_next, wg, wu, wd, tm):
    m, d = x.shape
    d_ff = wg.shape[1]
    tf = FFN_TF
    with_next = g_next is not None
    row = lambda i, j: (i, 0)
    vec = pl.BlockSpec((1, d), lambda i, j: (0, 0))
    x_spec = pl.BlockSpec((tm, d), row)
    w_specs = [pl.BlockSpec((d, tf), lambda i, j: (0, j)),
               pl.BlockSpec((d, tf), lambda i, j: (0, j)),
               pl.BlockSpec((tf, d), lambda i, j: (j, 0))]
    vecs = [g_pre, g_post] + ([g_next] if with_next else [])
    out_shape = [jax.ShapeDtypeStruct((m, d), F32)] + ([jax.ShapeDtypeStruct((m, d), BF16)] if with_next else [])
    res = pl.pallas_call(
        functools.partial(_ffn_kernel, d_ff=d_ff, with_next=with_next),
        out_shape=tuple(out_shape),
        grid=(m // tm, pl.cdiv(d_ff, tf)),
        in_specs=[x_spec] + [vec] * len(vecs) + w_specs,
        out_specs=tuple([x_spec] * len(out_shape)),
        scratch_shapes=[pltpu.VMEM((tm, d), BF16), pltpu.VMEM((tm, d), F32)],
        compiler_params=_cparams(("parallel", "arbitrary"), VMEM_LIMIT),
        name="ffn_half",
    )(x, *vecs, wg, wu, wd)
    return res if with_next else res[0]


def _proj_kernel(*refs, n_w, n_vec, epilogue):
    h = refs[0][...]
    accs = [jnp.dot(h, refs[1 + i][...], preferred_element_type=F32) for i in range(n_w)]
    vecs = [refs[1 + n_w + i][...] for i in range(n_vec)]
    outs = epilogue(*accs, *vecs)
    for o_ref, o in zip(refs[1 + n_w + n_vec:], outs):
        o_ref[...] = o.astype(o_ref.dtype)


def _proj(h, w, offs, n, vecs, epilogue, out_dtypes, tm, tn, name):
    m, k = h.shape
    assert all(o % tn == 0 for o in offs) and n % tn == 0
    kern = functools.partial(_proj_kernel, n_w=len(offs), n_vec=len(vecs), epilogue=epilogue)
    return pl.pallas_call(
        kern,
        out_shape=tuple(jax.ShapeDtypeStruct((m, n), dt) for dt in out_dtypes),
        grid=(m // tm, n // tn),
        in_specs=[pl.BlockSpec((tm, k), lambda i, j: (i, 0))]
        + [pl.BlockSpec((k, tn), functools.partial(lambda i, j, o: (0, o + j), o=off // tn))
           for off in offs]
        + [pl.BlockSpec((1, tn), lambda i, j: (0, j)) for _ in vecs],
        out_specs=tuple(pl.BlockSpec((tm, tn), lambda i, j: (i, j)) for _ in out_dtypes),
        compiler_params=_cparams(("parallel", "parallel"), VMEM_LIMIT),
        name=name,
    )(h, *([w] * len(offs)), *vecs)


def _ep_id(a):
    return (a,)


def _ep_glu(a, g):
    return (a * _sigmoid(g),)


def _ep_sigmoid(a):
    return (_sigmoid(a),)


def _ep_fox(q, k, v):
    return (q * (DH_F ** -0.5), k, k, v, v)


def _ep_logsig_bias(a, b):
    return (_log_sigmoid(a + b),)


def _ep_gla_gate(a, b):
    return (_log_sigmoid(a + b) * (1.0 / GATE_TAU),)


def _gla_tables(lc):
    nlev = int(math.log2(lc))
    t = np.arange(lc)
    rows = [np.tril(np.ones((lc, lc))),
            np.triu(np.ones((lc, lc)), 1)]
    masks, upper = [], []
    for lev in range(nlev):
        s = lc >> lev
        mid = (t // s) * s + s // 2
        up = t >= mid
        u = t[None, :]
        c = np.where(up[:, None], (u >= mid[:, None]) & (u <= t[:, None]),
                     (u > t[:, None]) & (u < mid[:, None]))
        rows.append(c.astype(np.float64))
        masks.append(((t[:, None] // s) == (t[None, :] // s)).astype(np.float32))
        upper.append(up.astype(np.float32))
    masks.append(np.eye(lc, dtype=np.float32))
    cmat = np.concatenate(rows, axis=0).astype(np.float32)
    return (jnp.asarray(cmat, BF16), jnp.asarray(np.stack(masks)),
            jnp.asarray(np.stack(upper)[:, :, None]))


def _gla_prompt_kernel(q_ref, k_ref, v_ref, r_ref, glr_ref, wg2_ref, bg_ref, gn_ref,
                       cmat_ref, mask_ref, up_ref, o_ref, s_ref, *, lc, nlev):
    c = pl.program_id(1)

    @pl.when(c == 0)
    def _():
        s_ref[...] = jnp.zeros_like(s_ref)

    x = jnp.dot(glr_ref[...], wg2_ref[...], preferred_element_type=F32) + bg_ref[...]
    la_all = _log_sigmoid(x) * (1.0 / GATE_TAU)
    g_all = _dot_exact_lhs(cmat_ref[...], la_all, 2)
    nt = (((1,), (1,)), ((), ()))

    for h in range(H_A):
        ks = slice(h * DK_HEAD_A, (h + 1) * DK_HEAD_A)
        vs = slice(h * DV_HEAD_A, (h + 1) * DV_HEAD_A)
        q = q_ref[:, ks].astype(F32) * (DK_HEAD_A ** -0.5)
        k = k_ref[:, ks].astype(F32)
        v = v_ref[:, vs]
        la = la_all[:, ks]
        b = g_all[0:lc, ks]
        rem = g_all[lc:2 * lc, ks]

        scores = lax.dot_general(q.astype(BF16), k.astype(BF16), nt,
                                 preferred_element_type=F32) * mask_ref[nlev]
        for lev in range(nlev):
            e = jnp.exp(g_all[(2 + lev) * lc:(3 + lev) * lc, ks])
            up = up_ref[lev]
            qe = (q * e * up).astype(BF16)
            ke = (k * e * (1.0 - up)).astype(BF16)
            scores = scores + lax.dot_general(qe, ke, nt, preferred_element_type=F32) * mask_ref[lev]

        s_old = s_ref[h]
        o = jnp.dot((q * jnp.exp(b)).astype(BF16), s_old.astype(BF16), preferred_element_type=F32)
        o = o + jnp.dot(scores.astype(BF16), v, preferred_element_type=F32)

        kd_t = (k * jnp.exp(rem)).T.astype(BF16)
        decay_col = jnp.exp(jnp.sum(la.T, axis=1, keepdims=True))
        s_ref[h] = decay_col * s_old + jnp.dot(kd_t, v, preferred_element_type=F32)

        r = r_ref[:, vs].astype(F32)
        o_ref[:, vs] = (_rms(o, gn_ref[...]) * (r * _sigmoid(r))).astype(o_ref.dtype)


def _gla_prompt(za, wg2p, bg, gn, batch, seq):
    lc, nlev = GLA_CHUNK, GLA_LEVELS
    nc = seq // lc
    cmat, masks, upper = _gla_tables(lc)
    rowi = lambda b, c: b * nc + c
    kern = functools.partial(_gla_prompt_kernel, lc=lc, nlev=nlev)
    const2 = lambda b, c: (0, 0)
    const3 = lambda b, c: (0, 0, 0)
    return pl.pallas_call(
        kern,
        out_shape=(jax.ShapeDtypeStruct((batch * seq, DV_A), BF16),
                   jax.ShapeDtypeStruct((batch, H_A, DK_HEAD_A, DV_HEAD_A), F32)),
        grid=(batch, nc),
        in_specs=[
            pl.BlockSpec((lc, DK_A), lambda b, c: (rowi(b, c), 0)),
            pl.BlockSpec((lc, DK_A), lambda b, c: (rowi(b, c), 1)),
            pl.BlockSpec((lc, DV_A), lambda b, c: (rowi(b, c), 1)),
            pl.BlockSpec((lc, DV_A), lambda b, c: (rowi(b, c), 2)),
            pl.BlockSpec((lc, LANES), lambda b, c: (rowi(b, c), (2 * DK_A + 2 * DV_A) // LANES)),
            pl.BlockSpec(wg2p.shape, const2),
            pl.BlockSpec(bg.shape, const2),
            pl.BlockSpec(gn.shape, const2),
            pl.BlockSpec(cmat.shape, const2),
            pl.BlockSpec(masks.shape, const3),
            pl.BlockSpec(upper.shape, const3),
        ],
        out_specs=(pl.BlockSpec((lc, DV_A), lambda b, c: (rowi(b, c), 0)),
                   pl.BlockSpec((None, H_A, DK_HEAD_A, DV_HEAD_A), lambda b, c: (b, 0, 0, 0))),
        compiler_params=_cparams(("parallel", "arbitrary")),
        name="gla_prompt",
    )(za, za, za, za, za, wg2p, bg, gn, cmat, masks, upper)


def _gla_sample_kernel(qc_ref, kc_ref, lac_ref, v_ref, r_ref, gn_ref, s_ref, o_ref, so_ref):
    for h in range(H_A):
        vs = slice(h * DV_HEAD_A, (h + 1) * DV_HEAD_A)
        s_new = jnp.exp(lac_ref[h]) * s_ref[h] + kc_ref[h] * v_ref[:, vs].astype(F32)
        so_ref[h] = s_new
        o = jnp.sum((qc_ref[h] * (DK_HEAD_A ** -0.5)) * s_new, axis=0, keepdims=True)
        r = r_ref[:, vs].astype(F32)
        o_ref[:, vs] = (_rms(o, gn_ref[...]) * (r * _sigmoid(r))).astype(o_ref.dtype)


def _gla_sample(za3, q_col, k_col, la_col, gn, state):
    nb = za3.shape[0]
    col = pl.BlockSpec((None, H_A, DK_HEAD_A, 1), lambda b: (b, 0, 0, 0))
    st = pl.BlockSpec((None, H_A, DK_HEAD_A, DV_HEAD_A), lambda b: (b, 0, 0, 0))
    return pl.pallas_call(
        _gla_sample_kernel,
        out_shape=(jax.ShapeDtypeStruct((nb, 1, DV_A), BF16),
                   jax.ShapeDtypeStruct(state.shape, F32)),
        grid=(nb,),
        in_specs=[col, col, col,
                  pl.BlockSpec((None, 1, DV_A), lambda b: (b, 0, 1)),
                  pl.BlockSpec((None, 1, DV_A), lambda b: (b, 0, 2)),
                  pl.BlockSpec((1, DV_HEAD_A), lambda b: (0, 0)),
                  st],
        out_specs=(pl.BlockSpec((None, 1, DV_A), lambda b: (b, 0, 0)), st),
        compiler_params=_cparams(("parallel",)),
        name="gla_sample",
    )(q_col, k_col, la_col, za3, za3, gn, state)


def _ln_swish(y, g, b):
    mu = jnp.mean(y, axis=-1, keepdims=True)
    yc = y - mu
    var = jnp.mean(yc * yc, axis=-1, keepdims=True)
    z = yc * lax.rsqrt(var + EPS) * g + b
    return z * _sigmoid(z)


def _conv_prompt_kernel(prev_ref, cur_ref, w_ref, bdw_ref, lng_ref, lnb_ref, o_ref, buf, ybuf, *, ts):
    t = pl.program_id(1)
    keep = (t > 0).astype(F32)
    buf[0:CONV_HALO, :] = prev_ref[...] * keep
    buf[CONV_HALO:, :] = cur_ref[...]
    first = CONV_HALO - (CONV_K - 1)
    rb = 64
    for cb in range(C_CONV // LANES):
        cs = slice(cb * LANES, (cb + 1) * LANES)
        for r0 in range(0, ts, rb):
            acc = jnp.zeros((rb, LANES), F32)
            for j in range(CONV_K):
                acc = acc + w_ref[j:j + 1, cs] * buf[r0 + first + j:r0 + first + j + rb, cs]
            ybuf[r0:r0 + rb, cs] = acc
    y = ybuf[...] + bdw_ref[...]
    o_ref[...] = _ln_swish(y, lng_ref[...], lnb_ref[...]).astype(o_ref.dtype)


def _conv_prompt(u, w_dw, b_dw, ln_g, ln_b, batch, seq):
    ts = CONV_TS
    nt = seq // ts
    per = ts // CONV_HALO
    vec = pl.BlockSpec((1, C_CONV), lambda b, t: (0, 0))
    kern = functools.partial(_conv_prompt_kernel, ts=ts)
    return pl.pallas_call(
        kern,
        out_shape=jax.ShapeDtypeStruct((batch * seq, C_CONV), BF16),
        grid=(batch, nt),
        in_specs=[pl.BlockSpec((CONV_HALO, C_CONV),
                               lambda b, t: (jnp.maximum((b * nt + t) * per - 1, 0), 0)),
                  pl.BlockSpec((ts, C_CONV), lambda b, t: (b * nt + t, 0)),
                  pl.BlockSpec((CONV_K, C_CONV), lambda b, t: (0, 0)),
                  vec, vec, vec],
        out_specs=pl.BlockSpec((ts, C_CONV), lambda b, t: (b * nt + t, 0)),
        scratch_shapes=[pltpu.VMEM((ts + CONV_HALO, C_CONV), F32), pltpu.VMEM((ts, C_CONV), F32)],
        compiler_params=_cparams(("parallel", "parallel")),
        name="conv_prompt",
    )(u, u, w_dw, b_dw, ln_g, ln_b)


def _conv_sample_kernel(buf_ref, u_ref, w_ref, bdw_ref, lng_ref, lnb_ref, o_ref, nb_ref):
    buf = buf_ref[...]
    u = u_ref[...]
    y = (jnp.sum(buf * w_ref[0:CONV_K - 1, :], axis=0, keepdims=True)
         + u * w_ref[CONV_K - 1:CONV_K, :] + bdw_ref[...])
    o_ref[...] = _ln_swish(y, lng_ref[...], lnb_ref[...]).astype(o_ref.dtype)
    nb_ref[0:CONV_K - 2, :] = buf[1:CONV_K - 1, :]
    nb_ref[CONV_K - 2:CONV_K - 1, :] = u


def _conv_sample(state, u3, w_dw, b_dw, ln_g, ln_b):
    nb = state.shape[0]
    vec = pl.BlockSpec((1, C_CONV), lambda b: (0, 0))
    st = pl.BlockSpec((None, CONV_K - 1, C_CONV), lambda b: (b, 0, 0))
    row = pl.BlockSpec((None, 1, C_CONV), lambda b: (b, 0, 0))
    return pl.pallas_call(
        _conv_sample_kernel,
        out_shape=(jax.ShapeDtypeStruct((nb, 1, C_CONV), BF16),
                   jax.ShapeDtypeStruct(state.shape, F32)),
        grid=(nb,),
        in_specs=[st, row, pl.BlockSpec((CONV_K, C_CONV), lambda b: (0, 0)), vec, vec, vec],
        out_specs=(row, st),
        compiler_params=_cparams(("parallel",)),
        name="conv_sample",
    )(state, u3, w_dw, b_dw, ln_g, ln_b)


def _fox_cum_kernel(h_ref, wt_ref, bcol_ref, c_ref):
    nt = (((1,), (1,)), ((), ()))
    fl = lax.dot_general(wt_ref[...], h_ref[...], nt, preferred_element_type=F32)
    x = _log_sigmoid(fl + bcol_ref[...])
    n = x.shape[1]
    lane = lax.broadcasted_iota(jnp.int32, x.shape, 1)
    sh = 1
    while sh < n:
        x = x + jnp.where(lane >= sh, pltpu.roll(x, sh, axis=1), 0.0)
        sh *= 2
    c_ref[...] = x[0:H_F]


def _fox_cum(h, w_fl_t, b_col, batch, seq):
    d = h.shape[1]
    return pl.pallas_call(
        _fox_cum_kernel,
        out_shape=jax.ShapeDtypeStruct((batch, H_F, seq), F32),
        grid=(batch,),
        in_specs=[pl.BlockSpec((seq, d), lambda b: (b, 0)),
                  pl.BlockSpec(w_fl_t.shape, lambda b: (0, 0)),
                  pl.BlockSpec(b_col.shape, lambda b: (0, 0))],
        out_specs=pl.BlockSpec((None, H_F, seq), lambda b: (b, 0, 0)),
        compiler_params=_cparams(("parallel",), VMEM_LIMIT),
        name="fox_cum",
    )(h, w_fl_t, b_col)


def _fox_prompt_kernel(q_ref, k_ref, v_ref, c_ref, o_ref, m_scr, l_scr, acc_scr, *, tq, tk):
    qi = pl.program_id(1)
    ki = pl.program_id(2)

    @pl.when(ki == 0)
    def _():
        m_scr[...] = jnp.full_like(m_scr, NEG_BIG)
        l_scr[...] = jnp.zeros_like(l_scr)
        acc_scr[...] = jnp.zeros_like(acc_scr)

    @pl.when(ki <= qi)
    def _():
        nt = (((1,), (1,)), ((), ()))
        row = lax.broadcasted_iota(jnp.int32, (tq, tk), 0)
        col = lax.broadcasted_iota(jnp.int32, (tq, tk), 1)
        visible = (ki < qi) | (col <= row)
        for h in range(H_F):
            hs = slice(h * DH_F, (h + 1) * DH_F)
            s = lax.dot_general(q_ref[:, hs], k_ref[:, hs], nt, preferred_element_type=F32)
            s = jnp.where(visible, s - c_ref[h:h + 1, :], NEG_BIG)
            m_prev = m_scr[h]
            m_new = jnp.maximum(m_prev, jnp.max(s, axis=1, keepdims=True))
            alpha = jnp.exp(m_prev - m_new)
            p = jnp.exp(s - jnp.concatenate([m_new] * (tk // LANES), axis=1))
            l_scr[h] = alpha * l_scr[h] + jnp.sum(p, axis=1, keepdims=True)
            m_scr[h] = m_new
            acc_scr[:, hs] = alpha * acc_scr[:, hs] + jnp.dot(
                p.astype(BF16), v_ref[:, hs], preferred_element_type=F32)

    @pl.when(ki == qi)
    def _():
        for h in range(H_F):
            hs = slice(h * DH_F, (h + 1) * DH_F)
            o_ref[:, hs] = (acc_scr[:, hs] / l_scr[h]).astype(o_ref.dtype)


def _fox_prompt(q, k, v, c_t, batch, seq, tq=256, tk=256):
    nq, nk = seq // tq, seq // tk
    kern = functools.partial(_fox_prompt_kernel, tq=tq, tk=tk)
    kv = pl.BlockSpec((tk, D_F), lambda b, i, j: (b * nk + jnp.minimum(j, i), 0))
    return pl.pallas_call(
        kern,
        out_shape=jax.ShapeDtypeStruct((batch * seq, D_F), BF16),
        grid=(batch, nq, nk),
        in_specs=[pl.BlockSpec((tq, D_F), lambda b, i, j: (b * nq + i, 0)), kv, kv,
                  pl.BlockSpec((None, H_F, tk), lambda b, i, j: (b, 0, jnp.minimum(j, i)))],
        out_specs=pl.BlockSpec((tq, D_F), lambda b, i, j: (b * nq + i, 0)),
        scratch_shapes=[pltpu.VMEM((H_F, tq, LANES), F32), pltpu.VMEM((H_F, tq, LANES), F32),
                        pltpu.VMEM((tq, D_F), F32)],
        compiler_params=_cparams(("parallel", "parallel", "arbitrary")),
        name="fox_prompt",
    )(q, k, v, c_t)


def _fox_bias_kernel(pt_ref, lf_hbm, lfnew_ref, o_ref, buf, wsuf, wtot, sem, *,
                     layer_off, n_pages, nbb):
    step = pl.program_id(0)
    width = PAGE_SIZE * H_F

    @pl.when(step == 0)
    def _():
        f = lax.broadcasted_iota(jnp.int32, wsuf.shape, 0)
        c = lax.broadcasted_iota(jnp.int32, wsuf.shape, 1)
        same_head = (f % H_F) == (c % H_F)
        wtot[...] = jnp.where(same_head, 1.0, 0.0).astype(BF16)
        wsuf[...] = jnp.where(same_head & ((f // H_F) > (c // H_F)), 1.0, 0.0).astype(BF16)

    def page_copy(j, p):
        page = layer_off + pt_ref[step * nbb + j, p]
        return pltpu.make_async_copy(lf_hbm.at[pl.ds(page, 1)],
                                     buf.at[pl.ds(j * n_pages + p, 1)], sem)

    for j in range(nbb):
        for p in range(n_pages):
            page_copy(j, p).start()
    for j in range(nbb):
        for p in range(n_pages):
            page_copy(j, p).wait()

    rows = nbb * n_pages
    pages = buf[...]
    within = _dot_exact_rhs(pages, wsuf[...], 3)
    tot = _dot_exact_rhs(pages, wtot[...], 3)
    i0 = lax.broadcasted_iota(jnp.int32, (rows, rows), 0)
    i1 = lax.broadcasted_iota(jnp.int32, (rows, rows), 1)
    later = jnp.where((i1 > i0) & ((i0 // n_pages) == (i1 // n_pages)), 1.0, 0.0).astype(BF16)
    res = within + _dot_exact_lhs(later, tot, 3)
    o_ref[...] = res.reshape(nbb, n_pages, width) + lfnew_ref[...]


def _fox_bias(page_table, lf_flat, lfnew_rep, layer, n_pool):
    nb, n_pages = page_table.shape
    width = PAGE_SIZE * H_F
    nbb = math.gcd(nb, BIAS_BATCH_PER_STEP)
    kern = functools.partial(_fox_bias_kernel, layer_off=layer * n_pool, n_pages=n_pages, nbb=nbb)
    grid_spec = pltpu.PrefetchScalarGridSpec(
        num_scalar_prefetch=1,
        grid=(nb // nbb,),
        in_specs=[pl.BlockSpec(memory_space=pl.ANY),
                  pl.BlockSpec((nbb, 1, width), lambda s, pt: (s, 0, 0))],
        out_specs=pl.BlockSpec((nbb, n_pages, width), lambda s, pt: (s, 0, 0)),
        scratch_shapes=[pltpu.VMEM((nbb * n_pages, width), F32),
                        pltpu.VMEM((width, width), BF16), pltpu.VMEM((width, width), BF16),
                        pltpu.SemaphoreType.DMA(())],
    )
    return pl.pallas_call(
        kern,
        out_shape=jax.ShapeDtypeStruct((nb, n_pages, width), F32),
        grid_spec=grid_spec,
        compiler_params=_cparams(("arbitrary",)),
        name="fox_bias",
    )(page_table, lf_flat, lfnew_rep)


def _fox_sample_kernel(pt_ref, q_ref, kn_ref, vn_ref, bias_ref, k_hbm, v_hbm, o_ref,
                       kbuf, vbuf, sems, m_scr, l_scr, acc_scr, *, layer, gp, ng, total):
    b = pl.program_id(0)
    g = pl.program_id(1)
    t = b * ng + g
    slot = t % 2

    def page_copies(bb, gg, sl):
        out = []
        for i in range(gp):
            page = pt_ref[bb, gg * gp + i]
            out.append(pltpu.make_async_copy(k_hbm.at[layer, page], kbuf.at[sl, i], sems.at[0, sl]))
            out.append(pltpu.make_async_copy(v_hbm.at[layer, page], vbuf.at[sl, i], sems.at[1, sl]))
        return out

    @pl.when(t == 0)
    def _():
        for cp in page_copies(b, g, slot):
            cp.start()

    @pl.when(t + 1 < total)
    def _():
        t1 = t + 1
        for cp in page_copies(t1 // ng, t1 % ng, 1 - slot):
            cp.start()

    q = q_ref[...]

    @pl.when(g == 0)
    def _():
        s_new = jnp.sum(q.astype(F32) * kn_ref[...], axis=1, keepdims=True)
        m_scr[...] = jnp.broadcast_to(s_new, m_scr.shape)
        l_scr[...] = jnp.ones_like(l_scr)
        acc_scr[...] = vn_ref[...]

    for cp in page_copies(b, g, slot):
        cp.wait()

    rows = gp * PAGE_SIZE * H_F
    nt = (((1,), (1,)), ((), ()))
    kf = kbuf[slot].reshape(rows, DH_F).astype(BF16)
    vf = vbuf[slot].reshape(rows, DH_F).astype(BF16)
    s = lax.dot_general(q, kf, nt, preferred_element_type=F32)
    own = (lax.broadcasted_iota(jnp.int32, (H_F, rows), 1) % H_F
           == lax.broadcasted_iota(jnp.int32, (H_F, rows), 0))
    s = jnp.where(own, s + bias_ref[...], NEG_BIG)
    m_prev = m_scr[...]
    m_new = jnp.maximum(m_prev, jnp.max(s, axis=1, keepdims=True))
    alpha = jnp.exp(m_prev - m_new)
    pr = jnp.exp(s - m_new[:, 0:1])
    l_scr[...] = alpha * l_scr[...] + jnp.sum(pr, axis=1, keepdims=True)
    m_scr[...] = m_new
    acc_scr[...] = alpha * acc_scr[...] + jnp.dot(pr.astype(BF16), vf, preferred_element_type=F32)

    @pl.when(g == ng - 1)
    def _():
        o_ref[...] = (acc_scr[...] / l_scr[...]).astype(o_ref.dtype)


def _fox_sample(page_table, q3, kn3, vn3, cache_k, cache_v, bias, layer):
    nb, n_pages = page_table.shape
    gp = math.gcd(n_pages, FOX_PAGES_PER_STEP)
    ng = n_pages // gp
    width = gp * PAGE_SIZE * H_F
    bias4 = bias.reshape(nb, ng, 1, width)
    row = pl.BlockSpec((None, H_F, DH_F), lambda b, g, pt: (b, 0, 0))
    hbm = pl.BlockSpec(memory_space=pl.ANY)
    kern = functools.partial(_fox_sample_kernel, layer=layer, gp=gp, ng=ng, total=nb * ng)
    grid_spec = pltpu.PrefetchScalarGridSpec(
        num_scalar_prefetch=1,
        grid=(nb, ng),
        in_specs=[row, row, row,
                  pl.BlockSpec((None, None, 1, width), lambda b, g, pt: (b, g, 0, 0)),
                  hbm, hbm],
        out_specs=row,
        scratch_shapes=[pltpu.VMEM((2, gp, PAGE_SIZE, H_F, DH_F), F32),
                        pltpu.VMEM((2, gp, PAGE_SIZE, H_F, DH_F), F32),
                        pltpu.SemaphoreType.DMA((2, 2)),
                        pltpu.VMEM((H_F, LANES), F32), pltpu.VMEM((H_F, LANES), F32),
                        pltpu.VMEM((H_F, DH_F), F32)],
    )
    return pl.pallas_call(
        kern,
        out_shape=jax.ShapeDtypeStruct((nb, H_F, DH_F), BF16),
        grid_spec=grid_spec,
        compiler_params=_cparams(("arbitrary", "arbitrary"), VMEM_LIMIT),
        name="fox_sample",
    )(page_table, q3, kn3, vn3, bias4, cache_k, cache_v)


def _merge_kernel(x_ref, oa_ref, ob_ref, oc_ref, ga_ref, gb_ref, gc_ref,
                  wa_ref, wb_ref, wc_ref, wm_ref, gpost_ref, o_ref):
    m = ga_ref[...].astype(F32) * jnp.dot(oa_ref[...], wa_ref[...], preferred_element_type=F32)
    m = m + gb_ref[...].astype(F32) * jnp.dot(ob_ref[...], wb_ref[...], preferred_element_type=F32)
    m = m + gc_ref[...].astype(F32) * jnp.dot(oc_ref[...], wc_ref[...], preferred_element_type=F32)
    y = jnp.dot(m.astype(BF16), wm_ref[...], preferred_element_type=F32)
    o_ref[...] = x_ref[...] + _rms(y, gpost_ref[...])


def _merge(x, oa, ob, oc, gates, wa, wb, wc, wm, g_post, tm):
    m, d = x.shape
    row = lambda i: (i, 0)
    const = lambda i: (0, 0)
    once = pl.Buffered(1)
    return pl.pallas_call(
        _merge_kernel,
        out_shape=jax.ShapeDtypeStruct((m, d), F32),
        grid=(m // tm,),
        in_specs=[pl.BlockSpec((tm, d), row),
                  pl.BlockSpec((tm, DV_A), row), pl.BlockSpec((tm, C_CONV), row),
                  pl.BlockSpec((tm, D_F), row),
                  pl.BlockSpec((tm, d), lambda i: (i, 0)), pl.BlockSpec((tm, d), lambda i: (i, 1)),
                  pl.BlockSpec((tm, d), lambda i: (i, 2)),
                  pl.BlockSpec(wa.shape, const, pipeline_mode=once),
                  pl.BlockSpec(wb.shape, const, pipeline_mode=once),
                  pl.BlockSpec(wc.shape, const, pipeline_mode=once),
                  pl.BlockSpec(wm.shape, const, pipeline_mode=once),
                  pl.BlockSpec((1, d), const)],
        out_specs=pl.BlockSpec((tm, d), row),
        compiler_params=_cparams(("parallel",), VMEM_LIMIT),
        name="merge",
    )(x, oa, ob, oc, gates, gates, gates, wa, wb, wc, wm, g_post)


def _pad_cols(w, n):
    return jnp.pad(w, ((0, 0), (0, n - w.shape[1])))


def _pad_rows(w, n):
    return jnp.pad(w, ((0, n - w.shape[0]), (0, 0)))


def _round_up(n, m):
    return -(-n // m) * m


def kernel(x_prompt, x_sample, cache_k, cache_v, cache_logf, state_gla, state_conv, page_table,
           norm_ffn1_pre, norm_ffn1_post, w_ffn1_gate, w_ffn1_up, w_ffn1_down,
           norm_mix_pre, norm_mix_post, w_in, w_gla_gate2, b_gla_gate, gla_norm, w_gla_out,
           w_dw, b_dw, conv_ln_g, conv_ln_b, w_conv_out, b_forget, w_fox_out, w_merge_out,
           norm_ffn2_pre, norm_ffn2_post, w_ffn2_gate, w_ffn2_up, w_ffn2_down):
    batch, seq, d = x_prompt.shape
    nb = x_sample.shape[0]
    depth = w_in.shape[0]
    n_pool = cache_k.shape[1]
    tm_p = min(512, batch * seq)
    tm_proj = min(1024, batch * seq)

    xp = x_prompt.reshape(batch * seq, d)
    xs = x_sample.reshape(nb, d)
    lf_flat = cache_logf.reshape(depth * n_pool, PAGE_SIZE * H_F)

    o_a = 2 * DK_A + 2 * DV_A + GATE_RANK
    o_u = o_a + 2 * C_CONV
    o_v = o_u + 3 * D_F
    o_f = o_v + H_F
    n_gl = w_in.shape[2] - o_f
    za_w = 2 * DK_A + 2 * DV_A + LANES
    p_gl = _round_up(za_w, PROJ_TN)
    p_q = p_gl + n_gl
    p_ca = p_q + 3 * D_F
    p_fl = p_ca + 2 * C_CONV
    assert p_q % PROJ_TN == 0 and p_ca % PROJ_TN == 0 and p_fl % LANES == 0

    outs = {n: [] for n in ("kp", "vp", "lfp", "sp", "bp", "ks", "vs", "lfs", "ss", "bs")}
    for l in range(depth):
        row = lambda a: a[l].reshape(1, -1)
        ffn1 = (w_ffn1_gate[l].astype(BF16), w_ffn1_up[l].astype(BF16), w_ffn1_down[l].astype(BF16))
        ffn2 = (w_ffn2_gate[l].astype(BF16), w_ffn2_up[l].astype(BF16), w_ffn2_down[l].astype(BF16))
        wi = w_in[l]
        zeros = lambda n: jnp.zeros((d, n), wi.dtype)
        w_all = jnp.concatenate(
            [wi[:, :o_a], zeros(p_gl - o_a), wi[:, o_f:], wi[:, o_u:o_v], wi[:, o_a:o_u],
             wi[:, o_v:o_f], zeros(LANES - H_F)], axis=1).astype(BF16)
        w_fl_t = _pad_rows(wi[:, o_v:o_f].T, 16).astype(BF16)
        wg2p = _pad_rows(w_gla_gate2[l], LANES).astype(BF16)
        bg = row(b_gla_gate)
        gn = row(gla_norm)
        bf_row = _pad_cols(row(b_forget), LANES)
        bf_col = _pad_rows(b_forget[l].reshape(-1, 1), 16)
        w_ao, w_bo, w_co, w_mo = (w_gla_out[l].astype(BF16), w_conv_out[l].astype(BF16),
                                  w_fox_out[l].astype(BF16), w_merge_out[l].astype(BF16))

        def project(h, tm):
            (za,) = _proj(h, w_all, [0], za_w, [], _ep_id, [BF16], tm, 640, "proj_gla")
            (u,) = _proj(h, w_all, [p_ca, p_ca + C_CONV], C_CONV, [], _ep_glu, [F32], tm, PROJ_TN,
                         "proj_conv")
            q, k32, k16, v32, v16 = _proj(h, w_all, [p_q, p_q + D_F, p_q + 2 * D_F], D_F, [], _ep_fox,
                                          [BF16, F32, BF16, F32, BF16], tm, PROJ_TN, "proj_fox")
            (lf,) = _proj(h, w_all, [p_fl], LANES, [bf_row], _ep_logsig_bias, [F32], tm, LANES,
                          "proj_logf")
            (gates,) = _proj(h, w_all, [p_gl], n_gl, [], _ep_sigmoid, [BF16], tm, PROJ_TN, "proj_gates")
            return za, u, q, k32, k16, v32, v16, lf[:, :H_F], gates

        xp, hp = _ffn(xp, row(norm_ffn1_pre), row(norm_ffn1_post), row(norm_mix_pre), *ffn1, tm_p)
        xs, hs = _ffn(xs, row(norm_ffn1_pre), row(norm_ffn1_post), row(norm_mix_pre), *ffn1, nb)

        za, u, q, k32, k16, v32, v16, lf, gates = project(hp, tm_proj)
        oa, s_fin = _gla_prompt(za, wg2p, bg, gn, batch, seq)
        ob = _conv_prompt(u, w_dw[l], row(b_dw), row(conv_ln_g), row(conv_ln_b), batch, seq)
        c_t = _fox_cum(hp, w_fl_t, bf_col, batch, seq)
        oc = _fox_prompt(q, k16, v16, c_t, batch, seq)
        xp = _merge(xp, oa, ob, oc, gates, w_ao, w_bo, w_co, w_mo, row(norm_mix_post),
                    min(256, batch * seq))
        outs["kp"].append(k32.reshape(batch, seq, H_F, DH_F))
        outs["vp"].append(v32.reshape(batch, seq, H_F, DH_F))
        outs["lfp"].append(lf.reshape(batch, seq, H_F))
        outs["sp"].append(s_fin)
        outs["bp"].append(u.reshape(batch, seq, C_CONV)[:, seq - (CONV_K - 1):])

        za, u, q, k32, k16, v32, v16, lf, gates = project(hs, nb)
        (la,) = _proj(za[:, 2 * DK_A + 2 * DV_A:], wg2p, [0], DK_A, [bg], _ep_gla_gate, [F32], nb, DK_A,
                      "gla_gate_sample")
        col = lambda a: a.astype(F32).reshape(nb, H_A, DK_HEAD_A, 1)
        oa, s_new = _gla_sample(za.reshape(nb, 1, za_w), col(za[:, :DK_A]),
                                col(za[:, DK_A:2 * DK_A]), col(la), gn, state_gla[l])
        ob, buf_new = _conv_sample(state_conv[l], u.reshape(nb, 1, C_CONV), w_dw[l], row(b_dw),
                                   row(conv_ln_g), row(conv_ln_b))
        lfnew_rep = jnp.tile(lf, (1, PAGE_SIZE)).reshape(nb, 1, PAGE_SIZE * H_F)
        bias = _fox_bias(page_table, lf_flat, lfnew_rep, l, n_pool)
        oc = _fox_sample(page_table, q.reshape(nb, H_F, DH_F), k32.reshape(nb, H_F, DH_F),
                         v32.reshape(nb, H_F, DH_F), cache_k, cache_v, bias, l)
        xs = _merge(xs, oa.reshape(nb, DV_A), ob.reshape(nb, C_CONV), oc.reshape(nb, D_F), gates,
                    w_ao, w_bo, w_co, w_mo, row(norm_mix_post), nb)
        outs["ks"].append(k32.reshape(nb, 1, H_F, DH_F))
        outs["vs"].append(v32.reshape(nb, 1, H_F, DH_F))
        outs["lfs"].append(lf.reshape(nb, 1, H_F))
        outs["ss"].append(s_new)
        outs["bs"].append(buf_new)

        xp = _ffn(xp, row(norm_ffn2_pre), row(norm_ffn2_post), None, *ffn2, tm_p)
        xs = _ffn(xs, row(norm_ffn2_pre), row(norm_ffn2_post), None, *ffn2, nb)

    st = lambda n: jnp.stack(outs[n])
    return (xp.reshape(batch, seq, d), xs.reshape(nb, 1, d),
            st("kp"), st("vp"), st("lfp"), st("sp"), st("bp"),
            st("ks"), st("vs"), st("lfs"), st("ss"), st("bs"))
```

```python
import functools
import math

import numpy as np
import jax
import jax.numpy as jnp
from jax import lax
from jax.experimental import pallas as pl
from jax.experimental.pallas import tpu as pltpu

F32 = jnp.float32
BF16 = jnp.bfloat16

H_A = 4
DK_HEAD_A = 128
DV_HEAD_A = 256
DK_A = H_A * DK_HEAD_A
DV_A = H_A * DV_HEAD_A
GATE_RANK = 16
GATE_TAU = 16.0
C_CONV = 1024
CONV_K = 31
H_F = 8
DH_F = 128
D_F = H_F * DH_F
PAGE_SIZE = 128
EPS = 1e-6

LANES = 128
V7X_VMEM_BYTES = 64 * 1024 * 1024
VMEM_LIMIT = V7X_VMEM_BYTES - 8 * 1024 * 1024

FFN_TF = 512
PROJ_TN = 512
GATES_TN = 1024
GLA_CHUNK = 128
GLA_LEVELS = 7
CONV_TS = 256
CONV_HALO = 32
FOX_PAGES_PER_STEP = 8
BIAS_BATCH_PER_STEP = 4
NEG_BIG = -1e30


def _cparams(sem, vmem=None):
    return pltpu.CompilerParams(dimension_semantics=sem, vmem_limit_bytes=vmem)


def _log_sigmoid(x):
    return jnp.minimum(x, 0.0) - jnp.log1p(jnp.exp(-jnp.abs(x)))


def _sigmoid(x):
    return 1.0 / (1.0 + jnp.exp(-x))


def _rms(x, g):
    return x * lax.rsqrt(jnp.mean(x * x, axis=-1, keepdims=True) + EPS) * g


def _bf16_pieces(a, n):
    parts, rem = [], a
    for i in range(n):
        p = rem.astype(BF16)
        parts.append(p)
        if i + 1 < n:
            rem = rem - p.astype(F32)
    return parts


def _dot_exact_rhs(a, w, n):
    return sum(jnp.dot(p, w, preferred_element_type=F32) for p in _bf16_pieces(a, n))


def _dot_exact_lhs(w, a, n):
    return sum(jnp.dot(w, p, preferred_element_type=F32) for p in _bf16_pieces(a, n))


def _ffn_kernel(*refs, d_ff, with_next):
    if with_next:
        (x_ref, gpre_ref, gpost_ref, gnext_ref, wg_ref, wu_ref, wd_ref,
         o_ref, hn_ref, h_scr, acc_scr) = refs
    else:
        x_ref, gpre_ref, gpost_ref, wg_ref, wu_ref, wd_ref, o_ref, h_scr, acc_scr = refs
    j = pl.program_id(1)
    tf = wg_ref.shape[1]

    @pl.when(j == 0)
    def _():
        h_scr[...] = _rms(x_ref[...], gpre_ref[...]).astype(BF16)
        acc_scr[...] = jnp.zeros_like(acc_scr)

    n_last = d_ff % tf
    assert n_last % LANES == 0

    def accumulate(n):
        h = h_scr[...]
        g = jnp.dot(h, wg_ref[:, 0:n], preferred_element_type=F32)
        u = jnp.dot(h, wu_ref[:, 0:n], preferred_element_type=F32)
        a = (g * _sigmoid(g) * u).astype(BF16)
        acc_scr[...] += jnp.dot(a, wd_ref[0:n, :], preferred_element_type=F32)

    if n_last == 0:
        accumulate(tf)
    else:
        pl.when(j < pl.num_programs(1) - 1)(lambda: accumulate(tf))
        pl.when(j == pl.num_programs(1) - 1)(lambda: accumulate(n_last))

    @pl.when(j == pl.num_programs(1) - 1)
    def _():
        xn = x_ref[...] + 0.5 * _rms(acc_scr[...], gpost_ref[...])
        o_ref[...] = xn
        if with_next:
            hn_ref[...] = _rms(xn, gnext_ref[...]).astype(BF16)


def _ffn(x, g_pre, g_post, g_next, wg, wu, wd, layer, tm):
    m, d = x.shape
    d_ff = wg.shape[2]
    tf = FFN_TF
    with_next = g_next is not None
    row = lambda i, j: (i, 0)
    vec = pl.BlockSpec((1, d), lambda i, j: (0, 0))
    x_spec = pl.BlockSpec((tm, d), row)
    w_specs = [pl.BlockSpec((None, d, tf), lambda i, j: (layer, 0, j)),
               pl.BlockSpec((None, d, tf), lambda i, j: (layer, 0, j)),
               pl.BlockSpec((None, tf, d), lambda i, j: (layer, j, 0))]
    vecs = [g_pre, g_post] + ([g_next] if with_next else [])
    out_shape = [jax.ShapeDtypeStruct((m, d), F32)] + ([jax.ShapeDtypeStruct((m, d), BF16)] if with_next else [])
    res = pl.pallas_call(
        functools.partial(_ffn_kernel, d_ff=d_ff, with_next=with_next),
        out_shape=tuple(out_shape),
        grid=(m // tm, pl.cdiv(d_ff, tf)),
        in_specs=[x_spec] + [vec] * len(vecs) + w_specs,
        out_specs=tuple([x_spec] * len(out_shape)),
        scratch_shapes=[pltpu.VMEM((tm, d), BF16), pltpu.VMEM((tm, d), F32)],
        compiler_params=_cparams(("parallel", "arbitrary"), VMEM_LIMIT),
        name="ffn_half",
    )(x, *vecs, wg, wu, wd)
    return res if with_next else res[0]


def _proj_kernel(*refs, n_w, n_vec, epilogue):
    h = refs[0][...]
    accs = [jnp.dot(h, refs[1 + i][...], preferred_element_type=F32) for i in range(n_w)]
    vecs = [refs[1 + n_w + i][...] for i in range(n_vec)]
    outs = epilogue(*accs, *vecs)
    for o_ref, o in zip(refs[1 + n_w + n_vec:], outs):
        o_ref[...] = o.astype(o_ref.dtype)


def _proj(h, w, layer, offs, n, vecs, epilogue, out_dtypes, tm, tn, name):
    m, k = h.shape
    assert all(o % tn == 0 for o in offs) and n % tn == 0
    kern = functools.partial(_proj_kernel, n_w=len(offs), n_vec=len(vecs), epilogue=epilogue)
    return pl.pallas_call(
        kern,
        out_shape=tuple(jax.ShapeDtypeStruct((m, n), dt) for dt in out_dtypes),
        grid=(m // tm, n // tn),
        in_specs=[pl.BlockSpec((tm, k), lambda i, j: (i, 0))]
        + [pl.BlockSpec((None, k, tn), functools.partial(lambda i, j, o: (layer, 0, o + j), o=off // tn))
           for off in offs]
        + [pl.BlockSpec((1, tn), lambda i, j: (0, j)) for _ in vecs],
        out_specs=tuple(pl.BlockSpec((tm, tn), lambda i, j: (i, j)) for _ in out_dtypes),
        compiler_params=_cparams(("parallel", "parallel"), VMEM_LIMIT),
        name=name,
    )(h, *([w] * len(offs)), *vecs)


def _ep_id(a):
    return (a,)


def _ep_glu(a, g):
    return (a * _sigmoid(g),)


def _ep_sigmoid(a):
    return (_sigmoid(a),)


def _ep_fox(q, k, v):
    return (q * (DH_F ** -0.5), k, k, v, v)


def _ep_logsig_bias(a, b):
    return (_log_sigmoid(a + b),)


def _ep_gla_gate(a, b):
    return (_log_sigmoid(a + b) * (1.0 / GATE_TAU),)


def _gla_tables(lc):
    nlev = int(math.log2(lc))
    t = np.arange(lc)
    rows = [np.tril(np.ones((lc, lc))),
            np.triu(np.ones((lc, lc)), 1)]
    masks, upper = [], []
    for lev in range(nlev):
        s = lc >> lev
        mid = (t // s) * s + s // 2
        up = t >= mid
        u = t[None, :]
        c = np.where(up[:, None], (u >= mid[:, None]) & (u <= t[:, None]),
                     (u > t[:, None]) & (u < mid[:, None]))
        rows.append(c.astype(np.float64))
        masks.append(((t[:, None] // s) == (t[None, :] // s)).astype(np.float32))
        upper.append(up.astype(np.float32))
    masks.append(np.eye(lc, dtype=np.float32))
    cmat = np.concatenate(rows, axis=0).astype(np.float32)
    return (jnp.asarray(cmat, BF16), jnp.asarray(np.stack(masks)),
            jnp.asarray(np.stack(upper)[:, :, None]))


def _gla_prompt_kernel(q_ref, k_ref, v_ref, r_ref, glr_ref, wg2_ref, bg_ref, gn_ref,
                       cmat_ref, mask_ref, up_ref, o_ref, s_ref, *, lc, nlev):
    c = pl.program_id(1)

    @pl.when(c == 0)
    def _():
        s_ref[...] = jnp.zeros_like(s_ref)

    x = jnp.dot(glr_ref[...], wg2_ref[...], preferred_element_type=F32) + bg_ref[...]
    la_all = _log_sigmoid(x) * (1.0 / GATE_TAU)
    g_all = _dot_exact_lhs(cmat_ref[...], la_all, 2)
    nt = (((1,), (1,)), ((), ()))

    for h in range(H_A):
        ks = slice(h * DK_HEAD_A, (h + 1) * DK_HEAD_A)
        vs = slice(h * DV_HEAD_A, (h + 1) * DV_HEAD_A)
        q = q_ref[:, ks].astype(F32) * (DK_HEAD_A ** -0.5)
        k = k_ref[:, ks].astype(F32)
        v = v_ref[:, vs]
        la = la_all[:, ks]
        b = g_all[0:lc, ks]
        rem = g_all[lc:2 * lc, ks]

        scores = lax.dot_general(q.astype(BF16), k.astype(BF16), nt,
                                 preferred_element_type=F32) * mask_ref[nlev]
        for lev in range(nlev):
            e = jnp.exp(g_all[(2 + lev) * lc:(3 + lev) * lc, ks])
            up = up_ref[lev]
            qe = (q * e * up).astype(BF16)
            ke = (k * e * (1.0 - up)).astype(BF16)
            scores = scores + lax.dot_general(qe, ke, nt, preferred_element_type=F32) * mask_ref[lev]

        s_old = s_ref[h]
        o = jnp.dot((q * jnp.exp(b)).astype(BF16), s_old.astype(BF16), preferred_element_type=F32)
        o = o + jnp.dot(scores.astype(BF16), v, preferred_element_type=F32)

        kd_t = (k * jnp.exp(rem)).T.astype(BF16)
        decay_col = jnp.exp(jnp.sum(la.T, axis=1, keepdims=True))
        s_ref[h] = decay_col * s_old + jnp.dot(kd_t, v, preferred_element_type=F32)

        r = r_ref[:, vs].astype(F32)
        o_ref[:, vs] = (_rms(o, gn_ref[...]) * (r * _sigmoid(r))).astype(o_ref.dtype)


def _gla_prompt(za, wg2p, bg, gn, batch, seq):
    lc, nlev = GLA_CHUNK, GLA_LEVELS
    nc = seq // lc
    cmat, masks, upper = _gla_tables(lc)
    rowi = lambda b, c: b * nc + c
    kern = functools.partial(_gla_prompt_kernel, lc=lc, nlev=nlev)
    const2 = lambda b, c: (0, 0)
    const3 = lambda b, c: (0, 0, 0)
    return pl.pallas_call(
        kern,
        out_shape=(jax.ShapeDtypeStruct((batch * seq, DV_A), BF16),
                   jax.ShapeDtypeStruct((batch, H_A, DK_HEAD_A, DV_HEAD_A), F32)),
        grid=(batch, nc),
        in_specs=[
            pl.BlockSpec((lc, DK_A), lambda b, c: (rowi(b, c), 0)),
            pl.BlockSpec((lc, DK_A), lambda b, c: (rowi(b, c), 1)),
            pl.BlockSpec((lc, DV_A), lambda b, c: (rowi(b, c), 1)),
            pl.BlockSpec((lc, DV_A), lambda b, c: (rowi(b, c), 2)),
            pl.BlockSpec((lc, LANES), lambda b, c: (rowi(b, c), (2 * DK_A + 2 * DV_A) // LANES)),
            pl.BlockSpec(wg2p.shape, const2),
            pl.BlockSpec(bg.shape, const2),
            pl.BlockSpec(gn.shape, const2),
            pl.BlockSpec(cmat.shape, const2),
            pl.BlockSpec(masks.shape, const3),
            pl.BlockSpec(upper.shape, const3),
        ],
        out_specs=(pl.BlockSpec((lc, DV_A), lambda b, c: (rowi(b, c), 0)),
                   pl.BlockSpec((None, H_A, DK_HEAD_A, DV_HEAD_A), lambda b, c: (b, 0, 0, 0))),
        compiler_params=_cparams(("parallel", "arbitrary")),
        name="gla_prompt",
    )(za, za, za, za, za, wg2p, bg, gn, cmat, masks, upper)


def _gla_sample_kernel(qc_ref, kc_ref, lac_ref, v_ref, r_ref, gn_ref, s_ref, o_ref, so_ref):
    for h in range(H_A):
        vs = slice(h * DV_HEAD_A, (h + 1) * DV_HEAD_A)
        s_new = jnp.exp(lac_ref[h]) * s_ref[h] + kc_ref[h] * v_ref[:, vs].astype(F32)
        so_ref[h] = s_new
        o = jnp.sum((qc_ref[h] * (DK_HEAD_A ** -0.5)) * s_new, axis=0, keepdims=True)
        r = r_ref[:, vs].astype(F32)
        o_ref[:, vs] = (_rms(o, gn_ref[...]) * (r * _sigmoid(r))).astype(o_ref.dtype)


def _gla_sample(za3, q_col, k_col, la_col, gn, state, layer):
    nb = za3.shape[0]
    col = pl.BlockSpec((None, H_A, DK_HEAD_A, 1), lambda b: (b, 0, 0, 0))
    st_in = pl.BlockSpec((None, None, H_A, DK_HEAD_A, DV_HEAD_A), lambda b: (layer, b, 0, 0, 0))
    st_out = pl.BlockSpec((None, H_A, DK_HEAD_A, DV_HEAD_A), lambda b: (b, 0, 0, 0))
    return pl.pallas_call(
        _gla_sample_kernel,
        out_shape=(jax.ShapeDtypeStruct((nb, 1, DV_A), BF16),
                   jax.ShapeDtypeStruct(state.shape[1:], F32)),
        grid=(nb,),
        in_specs=[col, col, col,
                  pl.BlockSpec((None, 1, DV_A), lambda b: (b, 0, 1)),
                  pl.BlockSpec((None, 1, DV_A), lambda b: (b, 0, 2)),
                  pl.BlockSpec((1, DV_HEAD_A), lambda b: (0, 0)),
                  st_in],
        out_specs=(pl.BlockSpec((None, 1, DV_A), lambda b: (b, 0, 0)), st_out),
        compiler_params=_cparams(("parallel",)),
        name="gla_sample",
    )(q_col, k_col, la_col, za3, za3, gn, state)


def _ln_swish(y, g, b):
    mu = jnp.mean(y, axis=-1, keepdims=True)
    yc = y - mu
    var = jnp.mean(yc * yc, axis=-1, keepdims=True)
    z = yc * lax.rsqrt(var + EPS) * g + b
    return z * _sigmoid(z)


def _conv_prompt_kernel(prev_ref, cur_ref, w_ref, bdw_ref, lng_ref, lnb_ref, o_ref, buf, ybuf, *, ts):
    t = pl.program_id(1)
    keep = (t > 0).astype(F32)
    buf[0:CONV_HALO, :] = prev_ref[...] * keep
    buf[CONV_HALO:, :] = cur_ref[...]
    first = CONV_HALO - (CONV_K - 1)
    rb = 64
    for cb in range(C_CONV // LANES):
        cs = slice(cb * LANES, (cb + 1) * LANES)
        for r0 in range(0, ts, rb):
            acc = jnp.zeros((rb, LANES), F32)
            for j in range(CONV_K):
                acc = acc + w_ref[j:j + 1, cs] * buf[r0 + first + j:r0 + first + j + rb, cs]
            ybuf[r0:r0 + rb, cs] = acc
    y = ybuf[...] + bdw_ref[...]
    o_ref[...] = _ln_swish(y, lng_ref[...], lnb_ref[...]).astype(o_ref.dtype)


def _conv_prompt(u, w_dw, b_dw, ln_g, ln_b, batch, seq):
    ts = CONV_TS
    nt = seq // ts
    per = ts // CONV_HALO
    vec = pl.BlockSpec((1, C_CONV), lambda b, t: (0, 0))
    kern = functools.partial(_conv_prompt_kernel, ts=ts)
    return pl.pallas_call(
        kern,
        out_shape=jax.ShapeDtypeStruct((batch * seq, C_CONV), BF16),
        grid=(batch, nt),
        in_specs=[pl.BlockSpec((CONV_HALO, C_CONV),
                               lambda b, t: (jnp.maximum((b * nt + t) * per - 1, 0), 0)),
                  pl.BlockSpec((ts, C_CONV), lambda b, t: (b * nt + t, 0)),
                  pl.BlockSpec((CONV_K, C_CONV), lambda b, t: (0, 0)),
                  vec, vec, vec],
        out_specs=pl.BlockSpec((ts, C_CONV), lambda b, t: (b * nt + t, 0)),
        scratch_shapes=[pltpu.VMEM((ts + CONV_HALO, C_CONV), F32), pltpu.VMEM((ts, C_CONV), F32)],
        compiler_params=_cparams(("parallel", "parallel")),
        name="conv_prompt",
    )(u, u, w_dw, b_dw, ln_g, ln_b)


def _conv_sample_kernel(buf_ref, u_ref, w_ref, bdw_ref, lng_ref, lnb_ref, o_ref, nb_ref):
    buf = buf_ref[...]
    u = u_ref[...]
    y = (jnp.sum(buf * w_ref[0:CONV_K - 1, :], axis=0, keepdims=True)
         + u * w_ref[CONV_K - 1:CONV_K, :] + bdw_ref[...])
    o_ref[...] = _ln_swish(y, lng_ref[...], lnb_ref[...]).astype(o_ref.dtype)
    nb_ref[0:CONV_K - 2, :] = buf[1:CONV_K - 1, :]
    nb_ref[CONV_K - 2:CONV_K - 1, :] = u


def _conv_sample(state, layer, u3, w_dw, b_dw, ln_g, ln_b):
    nb = state.shape[1]
    vec = pl.BlockSpec((1, C_CONV), lambda b: (0, 0))
    st_in = pl.BlockSpec((None, None, CONV_K - 1, C_CONV), lambda b: (layer, b, 0, 0))
    st_out = pl.BlockSpec((None, CONV_K - 1, C_CONV), lambda b: (b, 0, 0))
    row = pl.BlockSpec((None, 1, C_CONV), lambda b: (b, 0, 0))
    return pl.pallas_call(
        _conv_sample_kernel,
        out_shape=(jax.ShapeDtypeStruct((nb, 1, C_CONV), BF16),
                   jax.ShapeDtypeStruct(state.shape[1:], F32)),
        grid=(nb,),
        in_specs=[st_in, row, pl.BlockSpec((CONV_K, C_CONV), lambda b: (0, 0)), vec, vec, vec],
        out_specs=(row, st_out),
        compiler_params=_cparams(("parallel",)),
        name="conv_sample",
    )(state, u3, w_dw, b_dw, ln_g, ln_b)


def _fox_cum_kernel(h_ref, wt_ref, bcol_ref, c_ref):
    nt = (((1,), (1,)), ((), ()))
    fl = lax.dot_general(wt_ref[...], h_ref[...], nt, preferred_element_type=F32)
    x = _log_sigmoid(fl + bcol_ref[...])
    n = x.shape[1]
    lane = lax.broadcasted_iota(jnp.int32, x.shape, 1)
    sh = 1
    while sh < n:
        x = x + jnp.where(lane >= sh, pltpu.roll(x, sh, axis=1), 0.0)
        sh *= 2
    c_ref[...] = x[0:H_F]


def _fox_cum(h, w_fl_t, b_col, batch, seq):
    d = h.shape[1]
    return pl.pallas_call(
        _fox_cum_kernel,
        out_shape=jax.ShapeDtypeStruct((batch, H_F, seq), F32),
        grid=(batch,),
        in_specs=[pl.BlockSpec((seq, d), lambda b: (b, 0)),
                  pl.BlockSpec(w_fl_t.shape, lambda b: (0, 0)),
                  pl.BlockSpec(b_col.shape, lambda b: (0, 0))],
        out_specs=pl.BlockSpec((None, H_F, seq), lambda b: (b, 0, 0)),
        compiler_params=_cparams(("parallel",), VMEM_LIMIT),
        name="fox_cum",
    )(h, w_fl_t, b_col)


def _fox_prompt_kernel(q_ref, k_ref, v_ref, c_ref, o_ref, m_scr, l_scr, acc_scr, *, tq, tk):
    qi = pl.program_id(1)
    ki = pl.program_id(2)

    @pl.when(ki == 0)
    def _():
        m_scr[...] = jnp.full_like(m_scr, NEG_BIG)
        l_scr[...] = jnp.zeros_like(l_scr)
        acc_scr[...] = jnp.zeros_like(acc_scr)

    def tile(diagonal):
        nt = (((1,), (1,)), ((), ()))
        if diagonal:
            visible = (lax.broadcasted_iota(jnp.int32, (tq, tk), 1)
                       <= lax.broadcasted_iota(jnp.int32, (tq, tk), 0))
        for h in range(H_F):
            hs = slice(h * DH_F, (h + 1) * DH_F)
            s = lax.dot_general(q_ref[:, hs], k_ref[:, hs], nt, preferred_element_type=F32)
            s = s - c_ref[h:h + 1, :]
            if diagonal:
                s = jnp.where(visible, s, NEG_BIG)
            m_prev = m_scr[h]
            m_new = jnp.maximum(m_prev, jnp.max(s, axis=1, keepdims=True))
            alpha = jnp.exp(m_prev - m_new)
            p = jnp.exp(s - jnp.concatenate([m_new] * (tk // LANES), axis=1))
            l_scr[h] = alpha * l_scr[h] + jnp.sum(p, axis=1, keepdims=True)
            m_scr[h] = m_new
            acc_scr[:, hs] = alpha * acc_scr[:, hs] + jnp.dot(
                p.astype(BF16), v_ref[:, hs], preferred_element_type=F32)

    pl.when(ki < qi)(lambda: tile(False))
    pl.when(ki == qi)(lambda: tile(True))

    @pl.when(ki == qi)
    def _():
        for h in range(H_F):
            hs = slice(h * DH_F, (h + 1) * DH_F)
            o_ref[:, hs] = (acc_scr[:, hs] / l_scr[h]).astype(o_ref.dtype)


def _fox_prompt(q, k, v, c_t, batch, seq, tq=256, tk=256):
    nq, nk = seq // tq, seq // tk
    kern = functools.partial(_fox_prompt_kernel, tq=tq, tk=tk)
    kv = pl.BlockSpec((tk, D_F), lambda b, i, j: (b * nk + jnp.minimum(j, i), 0))
    return pl.pallas_call(
        kern,
        out_shape=jax.ShapeDtypeStruct((batch * seq, D_F), BF16),
        grid=(batch, nq, nk),
        in_specs=[pl.BlockSpec((tq, D_F), lambda b, i, j: (b * nq + i, 0)), kv, kv,
                  pl.BlockSpec((None, H_F, tk), lambda b, i, j: (b, 0, jnp.minimum(j, i)))],
        out_specs=pl.BlockSpec((tq, D_F), lambda b, i, j: (b * nq + i, 0)),
        scratch_shapes=[pltpu.VMEM((H_F, tq, LANES), F32), pltpu.VMEM((H_F, tq, LANES), F32),
                        pltpu.VMEM((tq, D_F), F32)],
        compiler_params=_cparams(("parallel", "parallel", "arbitrary")),
        name="fox_prompt",
    )(q, k, v, c_t)


def _fox_bias_kernel(pt_ref, lf_hbm, lfnew_ref, o_ref, buf, wsuf, wtot, sem, *,
                     layer_off, n_pages, nbb):
    step = pl.program_id(0)
    width = PAGE_SIZE * H_F

    @pl.when(step == 0)
    def _():
        f = lax.broadcasted_iota(jnp.int32, wsuf.shape, 0)
        c = lax.broadcasted_iota(jnp.int32, wsuf.shape, 1)
        same_head = (f % H_F) == (c % H_F)
        wtot[...] = jnp.where(same_head, 1.0, 0.0).astype(BF16)
        wsuf[...] = jnp.where(same_head & ((f // H_F) > (c // H_F)), 1.0, 0.0).astype(BF16)

    def page_copy(j, p):
        page = layer_off + pt_ref[step * nbb + j, p]
        return pltpu.make_async_copy(lf_hbm.at[pl.ds(page, 1)],
                                     buf.at[pl.ds(j * n_pages + p, 1)], sem)

    for j in range(nbb):
        for p in range(n_pages):
            page_copy(j, p).start()
    for j in range(nbb):
        for p in range(n_pages):
            page_copy(j, p).wait()

    rows = nbb * n_pages
    pages = buf[...]
    within = _dot_exact_rhs(pages, wsuf[...], 3)
    tot = _dot_exact_rhs(pages, wtot[...], 3)
    i0 = lax.broadcasted_iota(jnp.int32, (rows, rows), 0)
    i1 = lax.broadcasted_iota(jnp.int32, (rows, rows), 1)
    later = jnp.where((i1 > i0) & ((i0 // n_pages) == (i1 // n_pages)), 1.0, 0.0).astype(BF16)
    res = within + _dot_exact_lhs(later, tot, 3)
    o_ref[...] = res.reshape(nbb, n_pages, width) + lfnew_ref[...]


def _fox_bias(page_table, lf_flat, lfnew_rep, layer, n_pool):
    nb, n_pages = page_table.shape
    width = PAGE_SIZE * H_F
    nbb = math.gcd(nb, BIAS_BATCH_PER_STEP)
    kern = functools.partial(_fox_bias_kernel, layer_off=layer * n_pool, n_pages=n_pages, nbb=nbb)
    grid_spec = pltpu.PrefetchScalarGridSpec(
        num_scalar_prefetch=1,
        grid=(nb // nbb,),
        in_specs=[pl.BlockSpec(memory_space=pl.ANY),
                  pl.BlockSpec((nbb, 1, width), lambda s, pt: (s, 0, 0))],
        out_specs=pl.BlockSpec((nbb, n_pages, width), lambda s, pt: (s, 0, 0)),
        scratch_shapes=[pltpu.VMEM((nbb * n_pages, width), F32),
                        pltpu.VMEM((width, width), BF16), pltpu.VMEM((width, width), BF16),
                        pltpu.SemaphoreType.DMA(())],
    )
    return pl.pallas_call(
        kern,
        out_shape=jax.ShapeDtypeStruct((nb, n_pages, width), F32),
        grid_spec=grid_spec,
        compiler_params=_cparams(("arbitrary",)),
        name="fox_bias",
    )(page_table, lf_flat, lfnew_rep)


def _fox_sample_kernel(pt_ref, q_ref, kn_ref, vn_ref, bias_ref, k_hbm, v_hbm, o_ref,
                       kbuf, vbuf, sems, m_scr, l_scr, acc_scr, *, layer, gp, ng, total):
    b = pl.program_id(0)
    g = pl.program_id(1)
    t = b * ng + g
    slot = t % 2

    def page_copies(bb, gg, sl):
        out = []
        for i in range(gp):
            page = pt_ref[bb, gg * gp + i]
            out.append(pltpu.make_async_copy(k_hbm.at[layer, page], kbuf.at[sl, i], sems.at[0, sl]))
            out.append(pltpu.make_async_copy(v_hbm.at[layer, page], vbuf.at[sl, i], sems.at[1, sl]))
        return out

    @pl.when(t == 0)
    def _():
        for cp in page_copies(b, g, slot):
            cp.start()

    @pl.when(t + 1 < total)
    def _():
        t1 = t + 1
        for cp in page_copies(t1 // ng, t1 % ng, 1 - slot):
            cp.start()

    q = q_ref[...]

    @pl.when(g == 0)
    def _():
        s_new = jnp.sum(q.astype(F32) * kn_ref[...], axis=1, keepdims=True)
        m_scr[...] = jnp.broadcast_to(s_new, m_scr.shape)
        l_scr[...] = jnp.ones_like(l_scr)
        acc_scr[...] = vn_ref[...]

    for cp in page_copies(b, g, slot):
        cp.wait()

    rows = gp * PAGE_SIZE * H_F
    nt = (((1,), (1,)), ((), ()))
    kf = kbuf[slot].reshape(rows, DH_F).astype(BF16)
    vf = vbuf[slot].reshape(rows, DH_F).astype(BF16)
    s = lax.dot_general(q, kf, nt, preferred_element_type=F32)
    pw = PAGE_SIZE * H_F
    s = jnp.concatenate([s[:, i * pw:(i + 1) * pw] + bias_ref[i:i + 1, :] for i in range(gp)], axis=1)
    own = (lax.broadcasted_iota(jnp.int32, (H_F, rows), 1) % H_F
           == lax.broadcasted_iota(jnp.int32, (H_F, rows), 0))
    s = jnp.where(own, s, NEG_BIG)
    m_prev = m_scr[...]
    m_new = jnp.maximum(m_prev, jnp.max(s, axis=1, keepdims=True))
    alpha = jnp.exp(m_prev - m_new)
    pr = jnp.exp(s - m_new[:, 0:1])
    l_scr[...] = alpha * l_scr[...] + jnp.sum(pr, axis=1, keepdims=True)
    m_scr[...] = m_new
    acc_scr[...] = alpha * acc_scr[...] + jnp.dot(pr.astype(BF16), vf, preferred_element_type=F32)

    @pl.when(g == ng - 1)
    def _():
        o_ref[...] = (acc_scr[...] / l_scr[...]).astype(o_ref.dtype)


def _fox_sample(page_table, q3, kn3, vn3, cache_k, cache_v, bias, layer):
    nb, n_pages = page_table.shape
    gp = math.gcd(n_pages, FOX_PAGES_PER_STEP)
    ng = n_pages // gp
    row = pl.BlockSpec((None, H_F, DH_F), lambda b, g, pt: (b, 0, 0))
    hbm = pl.BlockSpec(memory_space=pl.ANY)
    kern = functools.partial(_fox_sample_kernel, layer=layer, gp=gp, ng=ng, total=nb * ng)
    grid_spec = pltpu.PrefetchScalarGridSpec(
        num_scalar_prefetch=1,
        grid=(nb, ng),
        in_specs=[row, row, row,
                  pl.BlockSpec((None, gp, PAGE_SIZE * H_F), lambda b, g, pt: (b, g, 0)),
                  hbm, hbm],
        out_specs=row,
        scratch_shapes=[pltpu.VMEM((2, gp, PAGE_SIZE, H_F, DH_F), F32),
                        pltpu.VMEM((2, gp, PAGE_SIZE, H_F, DH_F), F32),
                        pltpu.SemaphoreType.DMA((2, 2)),
                        pltpu.VMEM((H_F, LANES), F32), pltpu.VMEM((H_F, LANES), F32),
                        pltpu.VMEM((H_F, DH_F), F32)],
    )
    return pl.pallas_call(
        kern,
        out_shape=jax.ShapeDtypeStruct((nb, H_F, DH_F), BF16),
        grid_spec=grid_spec,
        compiler_params=_cparams(("arbitrary", "arbitrary"), VMEM_LIMIT),
        name="fox_sample",
    )(page_table, q3, kn3, vn3, bias, cache_k, cache_v)


def _merge_kernel(x_ref, oa_ref, ob_ref, oc_ref, ga_ref, gb_ref, gc_ref,
                  wa_ref, wb_ref, wc_ref, wm_ref, gpost_ref, o_ref):
    m = ga_ref[...].astype(F32) * jnp.dot(oa_ref[...], wa_ref[...], preferred_element_type=F32)
    m = m + gb_ref[...].astype(F32) * jnp.dot(ob_ref[...], wb_ref[...], preferred_element_type=F32)
    m = m + gc_ref[...].astype(F32) * jnp.dot(oc_ref[...], wc_ref[...], preferred_element_type=F32)
    y = jnp.dot(m.astype(BF16), wm_ref[...], preferred_element_type=F32)
    o_ref[...] = x_ref[...] + _rms(y, gpost_ref[...])


def _merge(x, oa, ob, oc, gates, wa, wb, wc, wm, g_post, layer, tm):
    m, d = x.shape
    row = lambda i: (i, 0)
    const = lambda i: (0, 0)
    once = pl.Buffered(1)
    w_spec = lambda w: pl.BlockSpec((None,) + w.shape[1:], lambda i: (layer, 0, 0), pipeline_mode=once)
    return pl.pallas_call(
        _merge_kernel,
        out_shape=jax.ShapeDtypeStruct((m, d), F32),
        grid=(m // tm,),
        in_specs=[pl.BlockSpec((tm, d), row),
                  pl.BlockSpec((tm, DV_A), row), pl.BlockSpec((tm, C_CONV), row),
                  pl.BlockSpec((tm, D_F), row),
                  pl.BlockSpec((tm, d), lambda i: (i, 0)), pl.BlockSpec((tm, d), lambda i: (i, 1)),
                  pl.BlockSpec((tm, d), lambda i: (i, 2)),
                  w_spec(wa), w_spec(wb), w_spec(wc), w_spec(wm),
                  pl.BlockSpec((1, d), const)],
        out_specs=pl.BlockSpec((tm, d), row),
        compiler_params=_cparams(("parallel",), VMEM_LIMIT),
        name="merge",
    )(x, oa, ob, oc, gates, gates, gates, wa, wb, wc, wm, g_post)


def _pad_cols(w, n):
    return jnp.pad(w, ((0, 0), (0, n - w.shape[1])))


def _pad_rows(w, n):
    return jnp.pad(w, ((0, n - w.shape[0]), (0, 0)))


def _round_up(n, m):
    return -(-n // m) * m


def kernel(x_prompt, x_sample, cache_k, cache_v, cache_logf, state_gla, state_conv, page_table,
           norm_ffn1_pre, norm_ffn1_post, w_ffn1_gate, w_ffn1_up, w_ffn1_down,
           norm_mix_pre, norm_mix_post, w_in, w_gla_gate2, b_gla_gate, gla_norm, w_gla_out,
           w_dw, b_dw, conv_ln_g, conv_ln_b, w_conv_out, b_forget, w_fox_out, w_merge_out,
           norm_ffn2_pre, norm_ffn2_post, w_ffn2_gate, w_ffn2_up, w_ffn2_down):
    batch, seq, d = x_prompt.shape
    nb = x_sample.shape[0]
    depth = w_in.shape[0]
    n_pool = cache_k.shape[1]
    tm_p = min(512, batch * seq)
    tm_proj = min(1024, batch * seq)

    xp = x_prompt.reshape(batch * seq, d)
    xs = x_sample.reshape(nb, d)
    lf_flat = cache_logf.reshape(depth * n_pool, PAGE_SIZE * H_F)

    o_a = 2 * DK_A + 2 * DV_A + GATE_RANK
    o_u = o_a + 2 * C_CONV
    o_v = o_u + 3 * D_F
    o_f = o_v + H_F
    n_gl = w_in.shape[2] - o_f
    za_w = 2 * DK_A + 2 * DV_A + LANES
    p_gl = _round_up(za_w, GATES_TN)
    p_q = p_gl + n_gl
    p_ca = p_q + 3 * D_F
    p_fl = p_ca + 2 * C_CONV
    assert p_q % PROJ_TN == 0 and p_ca % PROJ_TN == 0 and p_fl % LANES == 0

    bf = lambda w: w.astype(BF16)
    zeros = lambda n: jnp.zeros((depth, d, n), w_in.dtype)
    w_all = bf(jnp.concatenate(
        [w_in[:, :, :o_a], zeros(p_gl - o_a), w_in[:, :, o_f:], w_in[:, :, o_u:o_v],
         w_in[:, :, o_a:o_u], w_in[:, :, o_v:o_f], zeros(LANES - H_F)], axis=2))
    ffn1 = (bf(w_ffn1_gate), bf(w_ffn1_up), bf(w_ffn1_down))
    ffn2 = (bf(w_ffn2_gate), bf(w_ffn2_up), bf(w_ffn2_down))
    w_out = (bf(w_gla_out), bf(w_conv_out), bf(w_fox_out), bf(w_merge_out))
    wg2p_all = bf(jnp.pad(w_gla_gate2, ((0, 0), (0, LANES - GATE_RANK), (0, 0))))

    outs = {n: [] for n in ("kp", "vp", "lfp", "sp", "bp", "ks", "vs", "lfs", "ss", "bs")}
    for l in range(depth):
        row = lambda a: a[l].reshape(1, -1)
        w_fl_t = _pad_rows(w_in[l, :, o_v:o_f].T, 16).astype(BF16)
        wg2p = wg2p_all[l]
        bg = row(b_gla_gate)
        gn = row(gla_norm)
        bf_row = _pad_cols(row(b_forget), LANES)
        bf_col = _pad_rows(b_forget[l].reshape(-1, 1), 16)

        def project(h, tm):
            (za,) = _proj(h, w_all, l, [0], za_w, [], _ep_id, [BF16], tm, 640, "proj_gla")
            (u,) = _proj(h, w_all, l, [p_ca, p_ca + C_CONV], C_CONV, [], _ep_glu, [F32], tm, PROJ_TN,
                         "proj_conv")
            q, k32, k16, v32, v16 = _proj(h, w_all, l, [p_q, p_q + D_F, p_q + 2 * D_F], D_F, [],
                                          _ep_fox, [BF16, F32, BF16, F32, BF16], tm, PROJ_TN, "proj_fox")
            (lf,) = _proj(h, w_all, l, [p_fl], LANES, [bf_row], _ep_logsig_bias, [F32], tm, LANES,
                          "proj_logf")
            (gates,) = _proj(h, w_all, l, [p_gl], n_gl, [], _ep_sigmoid, [BF16], tm, GATES_TN,
                             "proj_gates")
            return za, u, q, k32, k16, v32, v16, lf[:, :H_F], gates

        xp, hp = _ffn(xp, row(norm_ffn1_pre), row(norm_ffn1_post), row(norm_mix_pre), *ffn1, l, tm_p)
        xs, hs = _ffn(xs, row(norm_ffn1_pre), row(norm_ffn1_post), row(norm_mix_pre), *ffn1, l, nb)

        za, u, q, k32, k16, v32, v16, lf, gates = project(hp, tm_proj)
        oa, s_fin = _gla_prompt(za, wg2p, bg, gn, batch, seq)
        ob = _conv_prompt(u, w_dw[l], row(b_dw), row(conv_ln_g), row(conv_ln_b), batch, seq)
        c_t = _fox_cum(hp, w_fl_t, bf_col, batch, seq)
        oc = _fox_prompt(q, k16, v16, c_t, batch, seq)
        xp = _merge(xp, oa, ob, oc, gates, *w_out, row(norm_mix_post), l, min(256, batch * seq))
        outs["kp"].append(k32.reshape(batch, seq, H_F, DH_F))
        outs["vp"].append(v32.reshape(batch, seq, H_F, DH_F))
        outs["lfp"].append(lf.reshape(batch, seq, H_F))
        outs["sp"].append(s_fin)
        outs["bp"].append(u.reshape(batch, seq, C_CONV)[:, seq - (CONV_K - 1):])

        za, u, q, k32, k16, v32, v16, lf, gates = project(hs, nb)
        (la,) = _proj(za[:, 2 * DK_A + 2 * DV_A:], wg2p_all, l, [0], DK_A, [bg], _ep_gla_gate, [F32],
                      nb, DK_A, "gla_gate_sample")
        col = lambda a: a.astype(F32).reshape(nb, H_A, DK_HEAD_A, 1)
        oa, s_new = _gla_sample(za.reshape(nb, 1, za_w), col(za[:, :DK_A]),
                                col(za[:, DK_A:2 * DK_A]), col(la), gn, state_gla, l)
        ob, buf_new = _conv_sample(state_conv, l, u.reshape(nb, 1, C_CONV), w_dw[l], row(b_dw),
                                   row(conv_ln_g), row(conv_ln_b))
        lfnew_rep = jnp.tile(lf, (1, PAGE_SIZE)).reshape(nb, 1, PAGE_SIZE * H_F)
        bias = _fox_bias(page_table, lf_flat, lfnew_rep, l, n_pool)
        oc = _fox_sample(page_table, q.reshape(nb, H_F, DH_F), k32.reshape(nb, H_F, DH_F),
                         v32.reshape(nb, H_F, DH_F), cache_k, cache_v, bias, l)
        xs = _merge(xs, oa.reshape(nb, DV_A), ob.reshape(nb, C_CONV), oc.reshape(nb, D_F), gates,
                    *w_out, row(norm_mix_post), l, nb)
        outs["ks"].append(k32.reshape(nb, 1, H_F, DH_F))
        outs["vs"].append(v32.reshape(nb, 1, H_F, DH_F))
        outs["lfs"].append(lf.reshape(nb, 1, H_F))
        outs["ss"].append(s_new)
        outs["bs"].append(buf_new)

        xp = _ffn(xp, row(norm_ffn2_pre), row(norm_ffn2_post), None, *ffn2, l, tm_p)
        xs = _ffn(xs, row(norm_ffn2_pre), row(norm_ffn2_post), None, *ffn2, l, nb)

    st = lambda n: jnp.stack(outs[n])
    return (xp.reshape(batch, seq, d), xs.reshape(nb, 1, d),
            st("kp"), st("vp"), st("lfp"), st("sp"), st("bp"),
            st("ks"), st("vs"), st("lfs"), st("ss"), st("bs"))
```

```python
import functools
import math

import numpy as np
import jax
import jax.numpy as jnp
from jax import lax
from jax.experimental import pallas as pl
from jax.experimental.pallas import tpu as pltpu

F32 = jnp.float32
BF16 = jnp.bfloat16

H_A = 4
DK_HEAD_A = 128
DV_HEAD_A = 256
DK_A = H_A * DK_HEAD_A
DV_A = H_A * DV_HEAD_A
GATE_RANK = 16
GATE_TAU = 16.0
C_CONV = 1024
CONV_K = 31
H_F = 8
DH_F = 128
D_F = H_F * DH_F
PAGE_SIZE = 128
EPS = 1e-6

LANES = 128
V7X_VMEM_BYTES = 64 * 1024 * 1024
VMEM_LIMIT = V7X_VMEM_BYTES - 8 * 1024 * 1024

FFN_TF = 512
PROJ_TN = 512
GATES_TN = 1024
GLA_CHUNK = 128
GLA_LEVELS = 7
CONV_TS = 256
CONV_HALO = 32
FOX_PAGES_PER_STEP = 8
BIAS_BATCH_PER_STEP = 4
NEG_BIG = -1e30


def _cparams(sem, vmem=None):
    return pltpu.CompilerParams(dimension_semantics=sem, vmem_limit_bytes=vmem)


def _log_sigmoid(x):
    return jnp.minimum(x, 0.0) - jnp.log1p(jnp.exp(-jnp.abs(x)))


def _sigmoid(x):
    return 1.0 / (1.0 + jnp.exp(-x))


def _rms(x, g):
    return x * lax.rsqrt(jnp.mean(x * x, axis=-1, keepdims=True) + EPS) * g


def _bf16_pieces(a, n):
    parts, rem = [], a
    for i in range(n):
        p = rem.astype(BF16)
        parts.append(p)
        if i + 1 < n:
            rem = rem - p.astype(F32)
    return parts


def _dot_exact_rhs(a, w, n):
    return sum(jnp.dot(p, w, preferred_element_type=F32) for p in _bf16_pieces(a, n))


def _dot_exact_lhs(w, a, n):
    return sum(jnp.dot(w, p, preferred_element_type=F32) for p in _bf16_pieces(a, n))


def _ffn_kernel(*refs, d_ff, with_next):
    if with_next:
        (x_ref, gpre_ref, gpost_ref, gnext_ref, wg_ref, wu_ref, wd_ref,
         o_ref, hn_ref, h_scr, acc_scr) = refs
    else:
        x_ref, gpre_ref, gpost_ref, wg_ref, wu_ref, wd_ref, o_ref, h_scr, acc_scr = refs
    j = pl.program_id(1)
    tf = wg_ref.shape[1]

    @pl.when(j == 0)
    def _():
        h_scr[...] = _rms(x_ref[...], gpre_ref[...]).astype(BF16)
        acc_scr[...] = jnp.zeros_like(acc_scr)

    n_last = d_ff % tf
    assert n_last % LANES == 0

    def accumulate(n):
        h = h_scr[...]
        g = jnp.dot(h, wg_ref[:, 0:n], preferred_element_type=F32)
        u = jnp.dot(h, wu_ref[:, 0:n], preferred_element_type=F32)
        a = (g * _sigmoid(g) * u).astype(BF16)
        acc_scr[...] += jnp.dot(a, wd_ref[0:n, :], preferred_element_type=F32)

    if n_last == 0:
        accumulate(tf)
    else:
        pl.when(j < pl.num_programs(1) - 1)(lambda: accumulate(tf))
        pl.when(j == pl.num_programs(1) - 1)(lambda: accumulate(n_last))

    @pl.when(j == pl.num_programs(1) - 1)
    def _():
        xn = x_ref[...] + 0.5 * _rms(acc_scr[...], gpost_ref[...])
        o_ref[...] = xn
        if with_next:
            hn_ref[...] = _rms(xn, gnext_ref[...]).astype(BF16)


def _ffn(x, g_pre, g_post, g_next, wg, wu, wd, layer, tm):
    m, d = x.shape
    d_ff = wg.shape[2]
    tf = FFN_TF
    with_next = g_next is not None
    row = lambda i, j: (i, 0)
    vec = pl.BlockSpec((1, d), lambda i, j: (0, 0))
    x_spec = pl.BlockSpec((tm, d), row)
    w_specs = [pl.BlockSpec((None, d, tf), lambda i, j: (layer, 0, j)),
               pl.BlockSpec((None, d, tf), lambda i, j: (layer, 0, j)),
               pl.BlockSpec((None, tf, d), lambda i, j: (layer, j, 0))]
    vecs = [g_pre, g_post] + ([g_next] if with_next else [])
    out_shape = [jax.ShapeDtypeStruct((m, d), F32)] + ([jax.ShapeDtypeStruct((m, d), BF16)] if with_next else [])
    res = pl.pallas_call(
        functools.partial(_ffn_kernel, d_ff=d_ff, with_next=with_next),
        out_shape=tuple(out_shape),
        grid=(m // tm, pl.cdiv(d_ff, tf)),
        in_specs=[x_spec] + [vec] * len(vecs) + w_specs,
        out_specs=tuple([x_spec] * len(out_shape)),
        scratch_shapes=[pltpu.VMEM((tm, d), BF16), pltpu.VMEM((tm, d), F32)],
        compiler_params=_cparams(("parallel", "arbitrary"), VMEM_LIMIT),
        name="ffn_half",
    )(x, *vecs, wg, wu, wd)
    return res if with_next else res[0]


def _proj_kernel(*refs, n_w, n_vec, epilogue):
    h = refs[0][...]
    accs = [jnp.dot(h, refs[1 + i][...], preferred_element_type=F32) for i in range(n_w)]
    vecs = [refs[1 + n_w + i][...] for i in range(n_vec)]
    outs = epilogue(*accs, *vecs)
    for o_ref, o in zip(refs[1 + n_w + n_vec:], outs):
        o_ref[...] = o.astype(o_ref.dtype)


def _proj(h, w, layer, offs, n, vecs, epilogue, out_dtypes, tm, tn, name):
    m, k = h.shape
    assert all(o % tn == 0 for o in offs) and n % tn == 0
    kern = functools.partial(_proj_kernel, n_w=len(offs), n_vec=len(vecs), epilogue=epilogue)
    return pl.pallas_call(
        kern,
        out_shape=tuple(jax.ShapeDtypeStruct((m, n), dt) for dt in out_dtypes),
        grid=(m // tm, n // tn),
        in_specs=[pl.BlockSpec((tm, k), lambda i, j: (i, 0))]
        + [pl.BlockSpec((None, k, tn), functools.partial(lambda i, j, o: (layer, 0, o + j), o=off // tn))
           for off in offs]
        + [pl.BlockSpec((1, tn), lambda i, j: (0, j)) for _ in vecs],
        out_specs=tuple(pl.BlockSpec((tm, tn), lambda i, j: (i, j)) for _ in out_dtypes),
        compiler_params=_cparams(("parallel", "parallel"), VMEM_LIMIT),
        name=name,
    )(h, *([w] * len(offs)), *vecs)


def _ep_id(a):
    return (a,)


def _ep_glu(a, g):
    return (a * _sigmoid(g),)


def _ep_sigmoid(a):
    return (_sigmoid(a),)


def _ep_fox(q, k, v):
    return (q * (DH_F ** -0.5), k, k, v, v)


def _ep_logsig_bias(a, b):
    return (_log_sigmoid(a + b),)


def _ep_gla_gate(a, b):
    return (_log_sigmoid(a + b) * (1.0 / GATE_TAU),)


def _gla_tables(lc):
    nlev = int(math.log2(lc))
    t = np.arange(lc)
    rows = [np.tril(np.ones((lc, lc))),
            np.triu(np.ones((lc, lc)), 1)]
    masks, upper = [], []
    for lev in range(nlev):
        s = lc >> lev
        mid = (t // s) * s + s // 2
        up = t >= mid
        u = t[None, :]
        c = np.where(up[:, None], (u >= mid[:, None]) & (u <= t[:, None]),
                     (u > t[:, None]) & (u < mid[:, None]))
        rows.append(c.astype(np.float64))
        masks.append(((t[:, None] // s) == (t[None, :] // s)).astype(np.float32))
        upper.append(up.astype(np.float32))
    masks.append(np.eye(lc, dtype=np.float32))
    cmat = np.concatenate(rows, axis=0).astype(np.float32)
    return (jnp.asarray(cmat, BF16), jnp.asarray(np.stack(masks)),
            jnp.asarray(np.stack(upper)[:, :, None]))


def _gla_prompt_kernel(q_ref, k_ref, v_ref, r_ref, glr_ref, wg2_ref, bg_ref, gn_ref,
                       cmat_ref, mask_ref, up_ref, o_ref, s_ref, *, lc, nlev):
    c = pl.program_id(1)

    @pl.when(c == 0)
    def _():
        s_ref[...] = jnp.zeros_like(s_ref)

    x = jnp.dot(glr_ref[...], wg2_ref[...], preferred_element_type=F32) + bg_ref[...]
    la_all = _log_sigmoid(x) * (1.0 / GATE_TAU)
    g_all = _dot_exact_lhs(cmat_ref[...], la_all, 2)
    nt = (((1,), (1,)), ((), ()))

    for h in range(H_A):
        ks = slice(h * DK_HEAD_A, (h + 1) * DK_HEAD_A)
        vs = slice(h * DV_HEAD_A, (h + 1) * DV_HEAD_A)
        q = q_ref[:, ks].astype(F32) * (DK_HEAD_A ** -0.5)
        k = k_ref[:, ks].astype(F32)
        v = v_ref[:, vs]
        la = la_all[:, ks]
        b = g_all[0:lc, ks]
        rem = g_all[lc:2 * lc, ks]

        scores = lax.dot_general(q.astype(BF16), k.astype(BF16), nt,
                                 preferred_element_type=F32) * mask_ref[nlev]
        for lev in range(nlev):
            e = jnp.exp(g_all[(2 + lev) * lc:(3 + lev) * lc, ks])
            up = up_ref[lev]
            qe = (q * e * up).astype(BF16)
            ke = (k * e * (1.0 - up)).astype(BF16)
            scores = scores + lax.dot_general(qe, ke, nt, preferred_element_type=F32) * mask_ref[lev]

        s_old = s_ref[h]
        o = jnp.dot((q * jnp.exp(b)).astype(BF16), s_old.astype(BF16), preferred_element_type=F32)
        o = o + jnp.dot(scores.astype(BF16), v, preferred_element_type=F32)

        kd_t = (k * jnp.exp(rem)).T.astype(BF16)
        decay_col = jnp.exp(jnp.sum(la.T, axis=1, keepdims=True))
        s_ref[h] = decay_col * s_old + jnp.dot(kd_t, v, preferred_element_type=F32)

        r = r_ref[:, vs].astype(F32)
        o_ref[:, vs] = (_rms(o, gn_ref[...]) * (r * _sigmoid(r))).astype(o_ref.dtype)


def _gla_prompt(za, wg2p, bg, gn, batch, seq):
    lc, nlev = GLA_CHUNK, GLA_LEVELS
    nc = seq // lc
    cmat, masks, upper = _gla_tables(lc)
    rowi = lambda b, c: b * nc + c
    kern = functools.partial(_gla_prompt_kernel, lc=lc, nlev=nlev)
    const2 = lambda b, c: (0, 0)
    const3 = lambda b, c: (0, 0, 0)
    return pl.pallas_call(
        kern,
        out_shape=(jax.ShapeDtypeStruct((batch * seq, DV_A), BF16),
                   jax.ShapeDtypeStruct((batch, H_A, DK_HEAD_A, DV_HEAD_A), F32)),
        grid=(batch, nc),
        in_specs=[
            pl.BlockSpec((lc, DK_A), lambda b, c: (rowi(b, c), 0)),
            pl.BlockSpec((lc, DK_A), lambda b, c: (rowi(b, c), 1)),
            pl.BlockSpec((lc, DV_A), lambda b, c: (rowi(b, c), 1)),
            pl.BlockSpec((lc, DV_A), lambda b, c: (rowi(b, c), 2)),
            pl.BlockSpec((lc, LANES), lambda b, c: (rowi(b, c), (2 * DK_A + 2 * DV_A) // LANES)),
            pl.BlockSpec(wg2p.shape, const2),
            pl.BlockSpec(bg.shape, const2),
            pl.BlockSpec(gn.shape, const2),
            pl.BlockSpec(cmat.shape, const2),
            pl.BlockSpec(masks.shape, const3),
            pl.BlockSpec(upper.shape, const3),
        ],
        out_specs=(pl.BlockSpec((lc, DV_A), lambda b, c: (rowi(b, c), 0)),
                   pl.BlockSpec((None, H_A, DK_HEAD_A, DV_HEAD_A), lambda b, c: (b, 0, 0, 0))),
        compiler_params=_cparams(("parallel", "arbitrary")),
        name="gla_prompt",
    )(za, za, za, za, za, wg2p, bg, gn, cmat, masks, upper)


def _gla_sample_kernel(qc_ref, kc_ref, lac_ref, v_ref, r_ref, gn_ref, s_ref, o_ref, so_ref):
    for h in range(H_A):
        vs = slice(h * DV_HEAD_A, (h + 1) * DV_HEAD_A)
        s_new = jnp.exp(lac_ref[h]) * s_ref[h] + kc_ref[h] * v_ref[:, vs].astype(F32)
        so_ref[h] = s_new
        o = jnp.sum((qc_ref[h] * (DK_HEAD_A ** -0.5)) * s_new, axis=0, keepdims=True)
        r = r_ref[:, vs].astype(F32)
        o_ref[:, vs] = (_rms(o, gn_ref[...]) * (r * _sigmoid(r))).astype(o_ref.dtype)


def _gla_sample(za3, q_col, k_col, la_col, gn, state, layer):
    nb = za3.shape[0]
    col = pl.BlockSpec((None, H_A, DK_HEAD_A, 1), lambda b: (b, 0, 0, 0))
    st_in = pl.BlockSpec((None, None, H_A, DK_HEAD_A, DV_HEAD_A), lambda b: (layer, b, 0, 0, 0))
    st_out = pl.BlockSpec((None, H_A, DK_HEAD_A, DV_HEAD_A), lambda b: (b, 0, 0, 0))
    return pl.pallas_call(
        _gla_sample_kernel,
        out_shape=(jax.ShapeDtypeStruct((nb, 1, DV_A), BF16),
                   jax.ShapeDtypeStruct(state.shape[1:], F32)),
        grid=(nb,),
        in_specs=[col, col, col,
                  pl.BlockSpec((None, 1, DV_A), lambda b: (b, 0, 1)),
                  pl.BlockSpec((None, 1, DV_A), lambda b: (b, 0, 2)),
                  pl.BlockSpec((1, DV_HEAD_A), lambda b: (0, 0)),
                  st_in],
        out_specs=(pl.BlockSpec((None, 1, DV_A), lambda b: (b, 0, 0)), st_out),
        compiler_params=_cparams(("parallel",)),
        name="gla_sample",
    )(q_col, k_col, la_col, za3, za3, gn, state)


def _ln_swish(y, g, b):
    mu = jnp.mean(y, axis=-1, keepdims=True)
    yc = y - mu
    var = jnp.mean(yc * yc, axis=-1, keepdims=True)
    z = yc * lax.rsqrt(var + EPS) * g + b
    return z * _sigmoid(z)


def _conv_prompt_kernel(prev_ref, cur_ref, w_ref, bdw_ref, lng_ref, lnb_ref, o_ref, buf, ybuf, *, ts):
    t = pl.program_id(1)
    keep = (t > 0).astype(F32)
    buf[0:CONV_HALO, :] = prev_ref[...] * keep
    buf[CONV_HALO:, :] = cur_ref[...]
    first = CONV_HALO - (CONV_K - 1)
    rb = 64
    for cb in range(C_CONV // LANES):
        cs = slice(cb * LANES, (cb + 1) * LANES)
        for r0 in range(0, ts, rb):
            acc = jnp.zeros((rb, LANES), F32)
            for j in range(CONV_K):
                acc = acc + w_ref[j:j + 1, cs] * buf[r0 + first + j:r0 + first + j + rb, cs]
            ybuf[r0:r0 + rb, cs] = acc
    y = ybuf[...] + bdw_ref[...]
    o_ref[...] = _ln_swish(y, lng_ref[...], lnb_ref[...]).astype(o_ref.dtype)


def _conv_prompt(u, w_dw, b_dw, ln_g, ln_b, batch, seq):
    ts = CONV_TS
    nt = seq // ts
    per = ts // CONV_HALO
    vec = pl.BlockSpec((1, C_CONV), lambda b, t: (0, 0))
    kern = functools.partial(_conv_prompt_kernel, ts=ts)
    return pl.pallas_call(
        kern,
        out_shape=jax.ShapeDtypeStruct((batch * seq, C_CONV), BF16),
        grid=(batch, nt),
        in_specs=[pl.BlockSpec((CONV_HALO, C_CONV),
                               lambda b, t: (jnp.maximum((b * nt + t) * per - 1, 0), 0)),
                  pl.BlockSpec((ts, C_CONV), lambda b, t: (b * nt + t, 0)),
                  pl.BlockSpec((CONV_K, C_CONV), lambda b, t: (0, 0)),
                  vec, vec, vec],
        out_specs=pl.BlockSpec((ts, C_CONV), lambda b, t: (b * nt + t, 0)),
        scratch_shapes=[pltpu.VMEM((ts + CONV_HALO, C_CONV), F32), pltpu.VMEM((ts, C_CONV), F32)],
        compiler_params=_cparams(("parallel", "parallel")),
        name="conv_prompt",
    )(u, u, w_dw, b_dw, ln_g, ln_b)


def _conv_sample_kernel(buf_ref, u_ref, w_ref, bdw_ref, lng_ref, lnb_ref, o_ref, nb_ref):
    buf = buf_ref[...]
    u = u_ref[...]
    y = (jnp.sum(buf * w_ref[0:CONV_K - 1, :], axis=0, keepdims=True)
         + u * w_ref[CONV_K - 1:CONV_K, :] + bdw_ref[...])
    o_ref[...] = _ln_swish(y, lng_ref[...], lnb_ref[...]).astype(o_ref.dtype)
    nb_ref[0:CONV_K - 2, :] = buf[1:CONV_K - 1, :]
    nb_ref[CONV_K - 2:CONV_K - 1, :] = u


def _conv_sample(state, layer, u3, w_dw, b_dw, ln_g, ln_b):
    nb = state.shape[1]
    vec = pl.BlockSpec((1, C_CONV), lambda b: (0, 0))
    st_in = pl.BlockSpec((None, None, CONV_K - 1, C_CONV), lambda b: (layer, b, 0, 0))
    st_out = pl.BlockSpec((None, CONV_K - 1, C_CONV), lambda b: (b, 0, 0))
    row = pl.BlockSpec((None, 1, C_CONV), lambda b: (b, 0, 0))
    return pl.pallas_call(
        _conv_sample_kernel,
        out_shape=(jax.ShapeDtypeStruct((nb, 1, C_CONV), BF16),
                   jax.ShapeDtypeStruct(state.shape[1:], F32)),
        grid=(nb,),
        in_specs=[st_in, row, pl.BlockSpec((CONV_K, C_CONV), lambda b: (0, 0)), vec, vec, vec],
        out_specs=(row, st_out),
        compiler_params=_cparams(("parallel",)),
        name="conv_sample",
    )(state, u3, w_dw, b_dw, ln_g, ln_b)


def _fox_cum_kernel(h_ref, wt_ref, bcol_ref, c_ref):
    nt = (((1,), (1,)), ((), ()))
    fl = lax.dot_general(wt_ref[...], h_ref[...], nt, preferred_element_type=F32)
    x = _log_sigmoid(fl + bcol_ref[...])
    n = x.shape[1]
    lane = lax.broadcasted_iota(jnp.int32, x.shape, 1)
    sh = 1
    while sh < n:
        x = x + jnp.where(lane >= sh, pltpu.roll(x, sh, axis=1), 0.0)
        sh *= 2
    c_ref[...] = x[0:H_F]


def _fox_cum(h, w_fl_t, b_col, batch, seq):
    d = h.shape[1]
    return pl.pallas_call(
        _fox_cum_kernel,
        out_shape=jax.ShapeDtypeStruct((batch, H_F, seq), F32),
        grid=(batch,),
        in_specs=[pl.BlockSpec((seq, d), lambda b: (b, 0)),
                  pl.BlockSpec(w_fl_t.shape, lambda b: (0, 0)),
                  pl.BlockSpec(b_col.shape, lambda b: (0, 0))],
        out_specs=pl.BlockSpec((None, H_F, seq), lambda b: (b, 0, 0)),
        compiler_params=_cparams(("parallel",), VMEM_LIMIT),
        name="fox_cum",
    )(h, w_fl_t, b_col)


def _fox_prompt_kernel(q_ref, k_ref, v_ref, c_ref, o_ref, m_scr, l_scr, acc_scr, *, tq, tk):
    qi = pl.program_id(1)
    ki = pl.program_id(2)

    @pl.when(ki == 0)
    def _():
        m_scr[...] = jnp.full_like(m_scr, NEG_BIG)
        l_scr[...] = jnp.zeros_like(l_scr)
        acc_scr[...] = jnp.zeros_like(acc_scr)

    def tile(diagonal):
        nt = (((1,), (1,)), ((), ()))
        if diagonal:
            visible = (lax.broadcasted_iota(jnp.int32, (tq, tk), 1)
                       <= lax.broadcasted_iota(jnp.int32, (tq, tk), 0))
        for h in range(H_F):
            hs = slice(h * DH_F, (h + 1) * DH_F)
            s = lax.dot_general(q_ref[:, hs], k_ref[:, hs], nt, preferred_element_type=F32)
            s = s - c_ref[h:h + 1, :]
            if diagonal:
                s = jnp.where(visible, s, NEG_BIG)
            m_prev = m_scr[h]
            m_new = jnp.maximum(m_prev, jnp.max(s, axis=1, keepdims=True))
            alpha = jnp.exp(m_prev - m_new)
            p = jnp.exp(s - jnp.concatenate([m_new] * (tk // LANES), axis=1))
            l_scr[h] = alpha * l_scr[h] + jnp.sum(p, axis=1, keepdims=True)
            m_scr[h] = m_new
            acc_scr[:, hs] = alpha * acc_scr[:, hs] + jnp.dot(
                p.astype(BF16), v_ref[:, hs], preferred_element_type=F32)

    pl.when(ki < qi)(lambda: tile(False))
    pl.when(ki == qi)(lambda: tile(True))

    @pl.when(ki == qi)
    def _():
        for h in range(H_F):
            hs = slice(h * DH_F, (h + 1) * DH_F)
            o_ref[:, hs] = (acc_scr[:, hs] / l_scr[h]).astype(o_ref.dtype)


def _fox_prompt(q, k, v, c_t, batch, seq, tq=256, tk=256):
    nq, nk = seq // tq, seq // tk
    kern = functools.partial(_fox_prompt_kernel, tq=tq, tk=tk)
    kv = pl.BlockSpec((tk, D_F), lambda b, i, j: (b * nk + jnp.minimum(j, i), 0))
    return pl.pallas_call(
        kern,
        out_shape=jax.ShapeDtypeStruct((batch * seq, D_F), BF16),
        grid=(batch, nq, nk),
        in_specs=[pl.BlockSpec((tq, D_F), lambda b, i, j: (b * nq + i, 0)), kv, kv,
                  pl.BlockSpec((None, H_F, tk), lambda b, i, j: (b, 0, jnp.minimum(j, i)))],
        out_specs=pl.BlockSpec((tq, D_F), lambda b, i, j: (b * nq + i, 0)),
        scratch_shapes=[pltpu.VMEM((H_F, tq, LANES), F32), pltpu.VMEM((H_F, tq, LANES), F32),
                        pltpu.VMEM((tq, D_F), F32)],
        compiler_params=_cparams(("parallel", "parallel", "arbitrary")),
        name="fox_prompt",
    )(q, k, v, c_t)


def _fox_bias_kernel(pt_ref, lf_hbm, lfnew_ref, o_ref, buf, wsuf, wtot, sem, *,
                     layer_off, n_pages, nbb):
    step = pl.program_id(0)
    width = PAGE_SIZE * H_F

    @pl.when(step == 0)
    def _():
        f = lax.broadcasted_iota(jnp.int32, wsuf.shape, 0)
        c = lax.broadcasted_iota(jnp.int32, wsuf.shape, 1)
        same_head = (f % H_F) == (c % H_F)
        wtot[...] = jnp.where(same_head, 1.0, 0.0).astype(BF16)
        wsuf[...] = jnp.where(same_head & ((f // H_F) > (c // H_F)), 1.0, 0.0).astype(BF16)

    def page_copy(j, p):
        page = layer_off + pt_ref[step * nbb + j, p]
        return pltpu.make_async_copy(lf_hbm.at[pl.ds(page, 1)],
                                     buf.at[pl.ds(j * n_pages + p, 1)], sem)

    for j in range(nbb):
        for p in range(n_pages):
            page_copy(j, p).start()
    for j in range(nbb):
        for p in range(n_pages):
            page_copy(j, p).wait()

    rows = nbb * n_pages
    pages = buf[...]
    within = _dot_exact_rhs(pages, wsuf[...], 3)
    tot = _dot_exact_rhs(pages, wtot[...], 3)
    i0 = lax.broadcasted_iota(jnp.int32, (rows, rows), 0)
    i1 = lax.broadcasted_iota(jnp.int32, (rows, rows), 1)
    later = jnp.where((i1 > i0) & ((i0 // n_pages) == (i1 // n_pages)), 1.0, 0.0).astype(BF16)
    res = within + _dot_exact_lhs(later, tot, 3)
    o_ref[...] = res.reshape(nbb, n_pages, width) + lfnew_ref[...]


def _fox_bias(page_table, lf_flat, lfnew_rep, layer, n_pool):
    nb, n_pages = page_table.shape
    width = PAGE_SIZE * H_F
    nbb = math.gcd(nb, BIAS_BATCH_PER_STEP)
    kern = functools.partial(_fox_bias_kernel, layer_off=layer * n_pool, n_pages=n_pages, nbb=nbb)
    grid_spec = pltpu.PrefetchScalarGridSpec(
        num_scalar_prefetch=1,
        grid=(nb // nbb,),
        in_specs=[pl.BlockSpec(memory_space=pl.ANY),
                  pl.BlockSpec((nbb, 1, width), lambda s, pt: (s, 0, 0))],
        out_specs=pl.BlockSpec((nbb, n_pages, width), lambda s, pt: (s, 0, 0)),
        scratch_shapes=[pltpu.VMEM((nbb * n_pages, width), F32),
                        pltpu.VMEM((width, width), BF16), pltpu.VMEM((width, width), BF16),
                        pltpu.SemaphoreType.DMA(())],
    )
    return pl.pallas_call(
        kern,
        out_shape=jax.ShapeDtypeStruct((nb, n_pages, width), F32),
        grid_spec=grid_spec,
        compiler_params=_cparams(("arbitrary",)),
        name="fox_bias",
    )(page_table, lf_flat, lfnew_rep)


def _fox_sample_kernel(pt_ref, q_ref, kn_ref, vn_ref, bias_ref, k_hbm, v_hbm, o_ref,
                       kbuf, vbuf, sems, m_scr, l_scr, acc_scr, *, layer, gp, ng, total):
    b = pl.program_id(0)
    g = pl.program_id(1)
    t = b * ng + g
    slot = t % 2

    def page_copies(bb, gg, sl):
        out = []
        for i in range(gp):
            page = pt_ref[bb, gg * gp + i]
            out.append(pltpu.make_async_copy(k_hbm.at[layer, page], kbuf.at[sl, i], sems.at[0, sl]))
            out.append(pltpu.make_async_copy(v_hbm.at[layer, page], vbuf.at[sl, i], sems.at[1, sl]))
        return out

    @pl.when(t == 0)
    def _():
        for cp in page_copies(b, g, slot):
            cp.start()

    @pl.when(t + 1 < total)
    def _():
        t1 = t + 1
        for cp in page_copies(t1 // ng, t1 % ng, 1 - slot):
            cp.start()

    q = q_ref[...]

    @pl.when(g == 0)
    def _():
        s_new = jnp.sum(q.astype(F32) * kn_ref[...], axis=1, keepdims=True)
        m_scr[...] = jnp.broadcast_to(s_new, m_scr.shape)
        l_scr[...] = jnp.ones_like(l_scr)
        acc_scr[...] = vn_ref[...]

    for cp in page_copies(b, g, slot):
        cp.wait()

    rows = gp * PAGE_SIZE * H_F
    nt = (((1,), (1,)), ((), ()))
    kf = kbuf[slot].reshape(rows, DH_F).astype(BF16)
    vf = vbuf[slot].reshape(rows, DH_F).astype(BF16)
    s = lax.dot_general(q, kf, nt, preferred_element_type=F32)
    pw = PAGE_SIZE * H_F
    s = jnp.concatenate([s[:, i * pw:(i + 1) * pw] + bias_ref[i:i + 1, :] for i in range(gp)], axis=1)
    own = (lax.broadcasted_iota(jnp.int32, (H_F, rows), 1) % H_F
           == lax.broadcasted_iota(jnp.int32, (H_F, rows), 0))
    s = jnp.where(own, s, NEG_BIG)
    m_prev = m_scr[...]
    m_new = jnp.maximum(m_prev, jnp.max(s, axis=1, keepdims=True))
    alpha = jnp.exp(m_prev - m_new)
    pr = jnp.exp(s - m_new[:, 0:1])
    l_scr[...] = alpha * l_scr[...] + jnp.sum(pr, axis=1, keepdims=True)
    m_scr[...] = m_new
    acc_scr[...] = alpha * acc_scr[...] + jnp.dot(pr.astype(BF16), vf, preferred_element_type=F32)

    @pl.when(g == ng - 1)
    def _():
        o_ref[...] = (acc_scr[...] / l_scr[...]).astype(o_ref.dtype)


def _fox_sample(page_table, q3, kn3, vn3, cache_k, cache_v, bias, layer):
    nb, n_pages = page_table.shape
    gp = math.gcd(n_pages, FOX_PAGES_PER_STEP)
    ng = n_pages // gp
    row = pl.BlockSpec((None, H_F, DH_F), lambda b, g, pt: (b, 0, 0))
    hbm = pl.BlockSpec(memory_space=pl.ANY)
    kern = functools.partial(_fox_sample_kernel, layer=layer, gp=gp, ng=ng, total=nb * ng)
    grid_spec = pltpu.PrefetchScalarGridSpec(
        num_scalar_prefetch=1,
        grid=(nb, ng),
        in_specs=[row, row, row,
                  pl.BlockSpec((None, gp, PAGE_SIZE * H_F), lambda b, g, pt: (b, g, 0)),
                  hbm, hbm],
        out_specs=row,
        scratch_shapes=[pltpu.VMEM((2, gp, PAGE_SIZE, H_F, DH_F), F32),
                        pltpu.VMEM((2, gp, PAGE_SIZE, H_F, DH_F), F32),
                        pltpu.SemaphoreType.DMA((2, 2)),
                        pltpu.VMEM((H_F, LANES), F32), pltpu.VMEM((H_F, LANES), F32),
                        pltpu.VMEM((H_F, DH_F), F32)],
    )
    return pl.pallas_call(
        kern,
        out_shape=jax.ShapeDtypeStruct((nb, H_F, DH_F), BF16),
        grid_spec=grid_spec,
        compiler_params=_cparams(("arbitrary", "arbitrary"), VMEM_LIMIT),
        name="fox_sample",
    )(page_table, q3, kn3, vn3, bias, cache_k, cache_v)


def _merge_kernel(x_ref, oa_ref, ob_ref, oc_ref, ga_ref, gb_ref, gc_ref,
                  wa_ref, wb_ref, wc_ref, wm_ref, gpost_ref, o_ref):
    m = ga_ref[...].astype(F32) * jnp.dot(oa_ref[...], wa_ref[...], preferred_element_type=F32)
    m = m + gb_ref[...].astype(F32) * jnp.dot(ob_ref[...], wb_ref[...], preferred_element_type=F32)
    m = m + gc_ref[...].astype(F32) * jnp.dot(oc_ref[...], wc_ref[...], preferred_element_type=F32)
    y = jnp.dot(m.astype(BF16), wm_ref[...], preferred_element_type=F32)
    o_ref[...] = x_ref[...] + _rms(y, gpost_ref[...])


def _merge(x, oa, ob, oc, gates, wa, wb, wc, wm, g_post, layer, tm):
    m, d = x.shape
    row = lambda i: (i, 0)
    const = lambda i: (0, 0)
    once = pl.Buffered(1)
    w_spec = lambda w: pl.BlockSpec((None,) + w.shape[1:], lambda i: (layer, 0, 0), pipeline_mode=once)
    return pl.pallas_call(
        _merge_kernel,
        out_shape=jax.ShapeDtypeStruct((m, d), F32),
        grid=(m // tm,),
        in_specs=[pl.BlockSpec((tm, d), row),
                  pl.BlockSpec((tm, DV_A), row), pl.BlockSpec((tm, C_CONV), row),
                  pl.BlockSpec((tm, D_F), row),
                  pl.BlockSpec((tm, d), lambda i: (i, 0)), pl.BlockSpec((tm, d), lambda i: (i, 1)),
                  pl.BlockSpec((tm, d), lambda i: (i, 2)),
                  w_spec(wa), w_spec(wb), w_spec(wc), w_spec(wm),
                  pl.BlockSpec((1, d), const)],
        out_specs=pl.BlockSpec((tm, d), row),
        compiler_params=_cparams(("parallel",), VMEM_LIMIT),
        name="merge",
    )(x, oa, ob, oc, gates, gates, gates, wa, wb, wc, wm, g_post)


def _pad_cols(w, n):
    return jnp.pad(w, ((0, 0), (0, n - w.shape[1])))


def _pad_rows(w, n):
    return jnp.pad(w, ((0, n - w.shape[0]), (0, 0)))


def _round_up(n, m):
    return -(-n // m) * m


def kernel(x_prompt, x_sample, cache_k, cache_v, cache_logf, state_gla, state_conv, page_table,
           norm_ffn1_pre, norm_ffn1_post, w_ffn1_gate, w_ffn1_up, w_ffn1_down,
           norm_mix_pre, norm_mix_post, w_in, w_gla_gate2, b_gla_gate, gla_norm, w_gla_out,
           w_dw, b_dw, conv_ln_g, conv_ln_b, w_conv_out, b_forget, w_fox_out, w_merge_out,
           norm_ffn2_pre, norm_ffn2_post, w_ffn2_gate, w_ffn2_up, w_ffn2_down):
    batch, seq, d = x_prompt.shape
    nb = x_sample.shape[0]
    depth = w_in.shape[0]
    n_pool = cache_k.shape[1]
    tm_p = min(512, batch * seq)
    tm_proj = min(1024, batch * seq)

    xp = x_prompt.reshape(batch * seq, d)
    xs = x_sample.reshape(nb, d)
    lf_flat = cache_logf.reshape(depth * n_pool, PAGE_SIZE * H_F)

    o_a = 2 * DK_A + 2 * DV_A + GATE_RANK
    o_u = o_a + 2 * C_CONV
    o_v = o_u + 3 * D_F
    o_f = o_v + H_F
    n_gl = w_in.shape[2] - o_f
    za_w = 2 * DK_A + 2 * DV_A + LANES
    p_gl = 0
    p_q = p_gl + n_gl
    p_ca = p_q + 3 * D_F
    p_fl = p_ca + 2 * C_CONV
    assert p_q % PROJ_TN == 0 and p_ca % PROJ_TN == 0 and p_fl % LANES == 0

    bf = lambda w: w.astype(BF16)
    w_plain = bf(w_in)
    w_all = jnp.concatenate(
        [w_plain[:, :, o_f:], w_plain[:, :, o_u:o_v], w_plain[:, :, o_a:o_u], w_plain[:, :, o_v:o_f],
         jnp.zeros((depth, d, LANES - H_F), BF16)], axis=2)
    ffn1 = (bf(w_ffn1_gate), bf(w_ffn1_up), bf(w_ffn1_down))
    ffn2 = (bf(w_ffn2_gate), bf(w_ffn2_up), bf(w_ffn2_down))
    w_out = (bf(w_gla_out), bf(w_conv_out), bf(w_fox_out), bf(w_merge_out))
    wg2p_all = bf(jnp.pad(w_gla_gate2, ((0, 0), (0, LANES - GATE_RANK), (0, 0))))

    outs = {n: [] for n in ("kp", "vp", "lfp", "sp", "bp", "ks", "vs", "lfs", "ss", "bs")}
    for l in range(depth):
        row = lambda a: a[l].reshape(1, -1)
        w_fl_t = _pad_rows(w_in[l, :, o_v:o_f].T, 16).astype(BF16)
        wg2p = wg2p_all[l]
        bg = row(b_gla_gate)
        gn = row(gla_norm)
        bf_row = _pad_cols(row(b_forget), LANES)
        bf_col = _pad_rows(b_forget[l].reshape(-1, 1), 16)

        def project(h, tm):
            (za,) = _proj(h, w_plain, l, [0], za_w, [], _ep_id, [BF16], tm, 640, "proj_gla")
            (u,) = _proj(h, w_all, l, [p_ca, p_ca + C_CONV], C_CONV, [], _ep_glu, [F32], tm, PROJ_TN,
                         "proj_conv")
            q, k32, k16, v32, v16 = _proj(h, w_all, l, [p_q, p_q + D_F, p_q + 2 * D_F], D_F, [],
                                          _ep_fox, [BF16, F32, BF16, F32, BF16], tm, PROJ_TN, "proj_fox")
            (lf,) = _proj(h, w_all, l, [p_fl], LANES, [bf_row], _ep_logsig_bias, [F32], tm, LANES,
                          "proj_logf")
            (gates,) = _proj(h, w_all, l, [p_gl], n_gl, [], _ep_sigmoid, [BF16], tm, GATES_TN,
                             "proj_gates")
            return za, u, q, k32, k16, v32, v16, lf[:, :H_F], gates

        xp, hp = _ffn(xp, row(norm_ffn1_pre), row(norm_ffn1_post), row(norm_mix_pre), *ffn1, l, tm_p)
        xs, hs = _ffn(xs, row(norm_ffn1_pre), row(norm_ffn1_post), row(norm_mix_pre), *ffn1, l, nb)

        za, u, q, k32, k16, v32, v16, lf, gates = project(hp, tm_proj)
        oa, s_fin = _gla_prompt(za, wg2p, bg, gn, batch, seq)
        ob = _conv_prompt(u, w_dw[l], row(b_dw), row(conv_ln_g), row(conv_ln_b), batch, seq)
        c_t = _fox_cum(hp, w_fl_t, bf_col, batch, seq)
        oc = _fox_prompt(q, k16, v16, c_t, batch, seq)
        xp = _merge(xp, oa, ob, oc, gates, *w_out, row(norm_mix_post), l, min(256, batch * seq))
        outs["kp"].append(k32.reshape(batch, seq, H_F, DH_F))
        outs["vp"].append(v32.reshape(batch, seq, H_F, DH_F))
        outs["lfp"].append(lf.reshape(batch, seq, H_F))
        outs["sp"].append(s_fin)
        outs["bp"].append(u.reshape(batch, seq, C_CONV)[:, seq - (CONV_K - 1):])

        za, u, q, k32, k16, v32, v16, lf, gates = project(hs, nb)
        (la,) = _proj(za[:, 2 * DK_A + 2 * DV_A:], wg2p_all, l, [0], DK_A, [bg], _ep_gla_gate, [F32],
                      nb, DK_A, "gla_gate_sample")
        col = lambda a: a.astype(F32).reshape(nb, H_A, DK_HEAD_A, 1)
        oa, s_new = _gla_sample(za.reshape(nb, 1, za_w), col(za[:, :DK_A]),
                                col(za[:, DK_A:2 * DK_A]), col(la), gn, state_gla, l)
        ob, buf_new = _conv_sample(state_conv, l, u.reshape(nb, 1, C_CONV), w_dw[l], row(b_dw),
                                   row(conv_ln_g), row(conv_ln_b))
        lfnew_rep = jnp.tile(lf, (1, PAGE_SIZE)).reshape(nb, 1, PAGE_SIZE * H_F)
        bias = _fox_bias(page_table, lf_flat, lfnew_rep, l, n_pool)
        oc = _fox_sample(page_table, q.reshape(nb, H_F, DH_F), k32.reshape(nb, H_F, DH_F),
                         v32.reshape(nb, H_F, DH_F), cache_k, cache_v, bias, l)
        xs = _merge(xs, oa.reshape(nb, DV_A), ob.reshape(nb, C_CONV), oc.reshape(nb, D_F), gates,
                    *w_out, row(norm_mix_post), l, nb)
        outs["ks"].append(k32.reshape(nb, 1, H_F, DH_F))
        outs["vs"].append(v32.reshape(nb, 1, H_F, DH_F))
        outs["lfs"].append(lf.reshape(nb, 1, H_F))
        outs["ss"].append(s_new)
        outs["bs"].append(buf_new)

        xp = _ffn(xp, row(norm_ffn2_pre), row(norm_ffn2_post), None, *ffn2, l, tm_p)
        xs = _ffn(xs, row(norm_ffn2_pre), row(norm_ffn2_post), None, *ffn2, l, nb)

    st = lambda n: jnp.stack(outs[n])
    return (xp.reshape(batch, seq, d), xs.reshape(nb, 1, d),
            st("kp"), st("vp"), st("lfp"), st("sp"), st("bp"),
            st("ks"), st("vs"), st("lfs"), st("ss"), st("bs"))
```
